```python
import math, functools
import jax, jax.numpy as jnp
from jax import lax
import numpy as np

D_MODEL = 4096
BATCH = 4
SEQ = 2048
DEPTH = 2
DEC_BATCH = 8
DEC_SEQ = 1
PAST_LEN = 16384
PAGE_SIZE = 128

D_MIX = D_MODEL
D_ATTN = D_MIX // 2
D_MLSTM = D_MIX // 4
D_RG = D_MIX - D_ATTN - D_MLSTM
DH_DA = 128
H_DA = D_ATTN // (2 * DH_DA)
H_M = 4
DH_M = D_MLSTM // H_M
RG_BLOCKS = 8
RG_BW = D_RG // RG_BLOCKS
CONV_W = 4
RG_C = 8.0
D_FF = 4 * D_MODEL
ROPE_THETA = 10000.0
Q_BLOCK = 128
MLSTM_CHUNK = 128
NORM_EPS = 1e-6
SUBLN_EPS = 1e-5
D_IN = 3 * D_ATTN + 4 * D_MLSTM + 2 * H_M + 2 * D_RG
_NEG = -1e30

kernel_name = 'hybrid_mlstm_diffattn_rglru_step'


def _split_points():
    sizes = [D_ATTN, D_ATTN, D_ATTN, D_MLSTM, D_MLSTM, D_MLSTM, D_MLSTM, H_M, H_M, D_RG, D_RG]
    pts, acc = [], 0
    for s in sizes[:-1]:
        acc += s
        pts.append(acc)
    return pts


def _rmsnorm(x, g, eps):
    xf = x.astype(jnp.float32)
    y = xf * lax.rsqrt(jnp.mean(xf * xf, axis=-1, keepdims=True) + eps)
    return (y * g.astype(jnp.float32)).astype(x.dtype)


def _rope(x, pos):
    inv = 1.0 / (ROPE_THETA ** (jnp.arange(0, DH_DA, 2, dtype=jnp.float32) / DH_DA))
    ang = pos.astype(jnp.float32)[:, None] * inv[None, :]
    ang = jnp.concatenate([ang, ang], axis=-1)
    cos = jnp.cos(ang)[:, None, None, :]
    sin = jnp.sin(ang)[:, None, None, :]
    xf = x.astype(jnp.float32)
    half = DH_DA // 2
    rot = jnp.concatenate([-xf[..., half:], xf[..., :half]], axis=-1)
    return (xf * cos + rot * sin).astype(x.dtype)


def _diff_attn_prompt(q, k, v, lam):
    B, S = q.shape[0], q.shape[1]
    nb = S // Q_BLOCK
    scale = DH_DA ** -0.5
    qb = jnp.moveaxis(q.reshape(B, nb, Q_BLOCK, H_DA, 2, DH_DA), 1, 0)
    kpos = jnp.arange(S)

    def block(args):
        qi, bi = args
        s = jnp.einsum('bqhcd,bkhcd->bchqk', qi, k).astype(jnp.float32) * scale
        qpos = bi * Q_BLOCK + jnp.arange(Q_BLOCK)
        s = jnp.where(kpos[None, :] <= qpos[:, None], s, _NEG)
        pr = jax.nn.softmax(s, axis=-1)
        a = (pr[:, 0] - lam * pr[:, 1]).astype(v.dtype)
        return jnp.einsum('bhqk,bkhe->bqhe', a, v)

    out = lax.map(block, (qb, jnp.arange(nb)))
    return jnp.moveaxis(out, 0, 1).reshape(B, S, H_DA, 2 * DH_DA)


def _diff_attn_sample(q, k, v, lam, k_past, v_past):
    T = q.shape[1]
    P = k_past.shape[1]
    scale = DH_DA ** -0.5
    s_past = jnp.einsum('bqhcd,bkhcd->bchqk', q, k_past).astype(jnp.float32) * scale
    s_new = jnp.einsum('bqhcd,bkhcd->bchqk', q, k).astype(jnp.float32) * scale
    causal = jnp.tril(jnp.ones((T, T), dtype=bool))
    s_new = jnp.where(causal, s_new, _NEG)
    pr = jax.nn.softmax(jnp.concatenate([s_past, s_new], axis=-1), axis=-1)
    a = (pr[:, 0] - lam * pr[:, 1]).astype(v.dtype)
    return (jnp.einsum('bhqk,bkhe->bqhe', a[..., :P], v_past)
            + jnp.einsum('bhqk,bkhe->bqhe', a[..., P:], v))


def _mlstm_chunk(carry, inp):
    C, n, m = carry
    q, k, v, ig, lf = inp
    L = q.shape[2]
    b = jnp.cumsum(lf, axis=-1)
    dm = b[..., :, None] - b[..., None, :] + ig[..., None, :]
    causal = jnp.tril(jnp.ones((L, L), dtype=bool))
    dm = jnp.where(causal, dm, -jnp.inf)
    inter = b + m[..., None]
    m_t = jnp.maximum(inter, jnp.max(dm, axis=-1))
    w_intra = jnp.exp(dm - m_t[..., None])
    w_inter = jnp.exp(inter - m_t)
    sc = w_intra * jnp.einsum('bhtd,bhsd->bhts', q, k)
    num = (jnp.einsum('bhts,bhsv->bhtv', sc, v)
           + w_inter[..., None] * jnp.einsum('bhvk,bhtk->bhtv', C, q))
    den = jnp.sum(sc, axis=-1) + w_inter * jnp.einsum('bhk,bhtk->bht', n, q)
    h = num / jnp.maximum(jnp.abs(den), jnp.exp(-m_t))[..., None]
    m_new = m_t[..., -1]
    decay = w_inter[..., -1]
    wk = jnp.exp(b[..., -1:] - b + ig - m_new[..., None])
    C_new = decay[..., None, None] * C + jnp.einsum('bhsv,bhsk->bhvk', wk[..., None] * v, k)
    n_new = decay[..., None] * n + jnp.einsum('bhs,bhsk->bhk', wk, k)
    return (C_new, n_new, m_new), h


def _mlstm_chunked(q, k, v, ig, lf):
    Bx, H, T, D = q.shape
    nc = T // MLSTM_CHUNK

    def to_chunks(z):
        return jnp.moveaxis(z.reshape((Bx, H, nc, MLSTM_CHUNK) + z.shape[3:]), 2, 0)

    init = (jnp.zeros((Bx, H, D, D), jnp.float32), jnp.zeros((Bx, H, D), jnp.float32),
            jnp.zeros((Bx, H), jnp.float32))
    state, hc = lax.scan(_mlstm_chunk, init,
                         (to_chunks(q), to_chunks(k), to_chunks(v), to_chunks(ig), to_chunks(lf)))
    return state, jnp.moveaxis(hc, 0, 2).reshape(Bx, H, T, D)


def _mlstm_step(q, k, v, ig, lf, state):
    C, n, m = state
    carry = (C.astype(jnp.float32), n.astype(jnp.float32), m.astype(jnp.float32))
    return _mlstm_chunk(carry, (q, k, v, ig, lf))


def _rglru(xr, gate, conv_prev, h0, conv_w, conv_b, w_ra, b_ra, w_rx, b_rx, lam):
    Bx, T, _ = xr.shape
    xp = jnp.concatenate([conv_prev.astype(xr.dtype), xr], axis=1)
    xc = conv_b + sum(xp[:, j:j + T] * conv_w[j] for j in range(CONV_W))
    new_conv = xp[:, -(CONV_W - 1):]
    xb = xc.reshape(Bx, T, RG_BLOCKS, RG_BW)
    r = jax.nn.sigmoid(jnp.einsum('btnc,ncd->btnd', xb, w_ra).reshape(Bx, T, D_RG) + b_ra)
    i = jax.nn.sigmoid(jnp.einsum('btnc,ncd->btnd', xb, w_rx).reshape(Bx, T, D_RG) + b_rx)
    log_a = -RG_C * r.astype(jnp.float32) * jax.nn.softplus(-lam.astype(jnp.float32))
    a = jnp.exp(log_a)
    mult = jnp.sqrt(-jnp.expm1(2.0 * log_a))
    bt = mult * i.astype(jnp.float32) * xc.astype(jnp.float32)
    bt = bt.at[:, 0].add(a[:, 0] * h0.astype(jnp.float32))

    def comb(lhs, rhs):
        a1, b1 = lhs
        a2, b2 = rhs
        return a1 * a2, a2 * b1 + b2

    _, h = lax.associative_scan(comb, (a, bt), axis=1)
    y = (h * jax.nn.gelu(gate.astype(jnp.float32))).astype(xr.dtype)
    return y, new_conv, h[:, -1]


def _head_layernorm(h, g):
    mu = jnp.mean(h, axis=-1, keepdims=True)
    hc = h - mu
    var = jnp.mean(hc * hc, axis=-1, keepdims=True)
    return hc * lax.rsqrt(var + NORM_EPS) * g.astype(jnp.float32).reshape(H_M, DH_M)


def _layer(x, pos, p, lam, lam_init, attn_core, mlstm_core, conv_prev, h0):
    Bx, Tx, _ = x.shape
    hn = _rmsnorm(x, p['norm_mix_g'], NORM_EPS)
    u = jnp.einsum('btd,de->bte', hn, p['w_in'])
    qa, ka, va, qm, km, vm, om, igp, fgp, rx, rgate = jnp.split(u, _split_points(), axis=-1)
    qa = _rope(qa.reshape(Bx, Tx, H_DA, 2, DH_DA), pos)
    ka = _rope(ka.reshape(Bx, Tx, H_DA, 2, DH_DA), pos)
    va = va.reshape(Bx, Tx, H_DA, 2 * DH_DA)
    att = attn_core(qa, ka, va, lam)
    att = (_rmsnorm(att, p['attn_subln_g'], SUBLN_EPS) * (1.0 - lam_init)).reshape(Bx, Tx, D_ATTN)
    def heads(z):
        return z.reshape(Bx, Tx, H_M, DH_M).transpose(0, 2, 1, 3).astype(jnp.float32)
    q = heads(qm)
    k = heads(km) * (DH_M ** -0.5)
    v = heads(vm)
    ig = (igp.astype(jnp.float32) + p['b_ig'].astype(jnp.float32)).transpose(0, 2, 1)
    lf = jax.nn.log_sigmoid(fgp.astype(jnp.float32) + p['b_fg'].astype(jnp.float32)).transpose(0, 2, 1)
    mstate, hm = mlstm_core(q, k, v, ig, lf)
    hm = _head_layernorm(hm.transpose(0, 2, 1, 3), p['mlstm_norm_g'])
    hm = (hm * jax.nn.sigmoid(om.astype(jnp.float32)).reshape(Bx, Tx, H_M, DH_M))
    hm = hm.reshape(Bx, Tx, D_MLSTM).astype(x.dtype)
    hr, conv_new, h_last = _rglru(rx, rgate, conv_prev, h0, p['conv_w'], p['conv_b'],
                                  p['w_ra'], p['b_ra'], p['w_rx'], p['b_rx'], p['rg_lambda'])
    mix = jnp.einsum('bte,ed->btd', jnp.concatenate([att, hm, hr], axis=-1), p['w_out'])
    x = x + mix
    hn2 = _rmsnorm(x, p['norm_mlp_g'], NORM_EPS)
    a = jax.nn.relu(jnp.einsum('btd,df->btf', hn2, p['w_up']))
    x = x + jnp.einsum('btf,fd->btd', a * a, p['w_down'])
    return x, ka.reshape(Bx, Tx, H_DA, 2 * DH_DA), va, mstate, conv_new, h_last


def setup_inputs(seed: int = 0) -> dict:
    key = jax.random.key(seed)
    ks = jax.random.split(key, 40)
    f32 = jnp.float32
    n_pages = PAST_LEN // PAGE_SIZE
    n_pool = (5 * DEC_BATCH * n_pages + 3) // 4
    nrm = lambda k, s, sc: jax.random.normal(k, s, f32) * sc
    a0 = jax.random.uniform(ks[30], (DEPTH, D_RG), f32, 0.9, 0.999)
    a_base = a0 ** (1.0 / RG_C)
    perm = jax.random.permutation(ks[5], n_pool)[:DEC_BATCH * n_pages]
    return {
        'x_prompt': nrm(ks[0], (BATCH, SEQ, D_MODEL), 1.0),
        'x_sample': nrm(ks[1], (DEC_BATCH, DEC_SEQ, D_MODEL), 1.0),
        'cache_k': nrm(ks[2], (DEPTH, n_pool, PAGE_SIZE, H_DA, 2 * DH_DA), 1.0),
        'cache_v': nrm(ks[3], (DEPTH, n_pool, PAGE_SIZE, H_DA, 2 * DH_DA), 1.0),
        'page_table': perm.reshape(DEC_BATCH, n_pages).astype(jnp.int32),
        'state_mlstm_c': nrm(ks[6], (DEPTH, DEC_BATCH, H_M, DH_M, DH_M), 0.1),
        'state_mlstm_n': nrm(ks[7], (DEPTH, DEC_BATCH, H_M, DH_M), 0.1),
        'state_mlstm_m': nrm(ks[8], (DEPTH, DEC_BATCH, H_M), 0.5),
        'state_conv': nrm(ks[9], (DEPTH, DEC_BATCH, CONV_W - 1, D_RG), 1.0),
        'state_rglru_h': nrm(ks[10], (DEPTH, DEC_BATCH, D_RG), 0.5),
        'norm_mix_g': 1.0 + nrm(ks[11], (DEPTH, D_MODEL), 0.02),
        'w_in': nrm(ks[12], (DEPTH, D_MODEL, D_IN), D_MODEL ** -0.5),
        'w_out': nrm(ks[13], (DEPTH, D_MIX, D_MODEL), D_MIX ** -0.5),
        'lam_q1': nrm(ks[14], (DEPTH, DH_DA), 0.1),
        'lam_k1': nrm(ks[15], (DEPTH, DH_DA), 0.1),
        'lam_q2': nrm(ks[16], (DEPTH, DH_DA), 0.1),
        'lam_k2': nrm(ks[17], (DEPTH, DH_DA), 0.1),
        'attn_subln_g': 1.0 + nrm(ks[18], (DEPTH, 2 * DH_DA), 0.02),
        'b_ig': nrm(ks[19], (DEPTH, H_M), 0.1),
        'b_fg': 3.0 + jax.random.uniform(ks[20], (DEPTH, H_M), f32, 0.0, 3.0),
        'mlstm_norm_g': 1.0 + nrm(ks[21], (DEPTH, D_MLSTM), 0.02),
        'conv_w': nrm(ks[22], (DEPTH, CONV_W, D_RG), CONV_W ** -0.5),
        'conv_b': nrm(ks[23], (DEPTH, D_RG), 0.01),
        'w_ra': nrm(ks[24], (DEPTH, RG_BLOCKS, RG_BW, RG_BW), RG_BW ** -0.5),
        'b_ra': nrm(ks[25], (DEPTH, D_RG), 0.01),
        'w_rx': nrm(ks[26], (DEPTH, RG_BLOCKS, RG_BW, RG_BW), RG_BW ** -0.5),
        'b_rx': nrm(ks[27], (DEPTH, D_RG), 0.01),
        'rg_lambda': jnp.log(a_base) - jnp.log1p(-a_base),
        'norm_mlp_g': 1.0 + nrm(ks[28], (DEPTH, D_MODEL), 0.02),
        'w_up': nrm(ks[29], (DEPTH, D_MODEL, D_FF), D_MODEL ** -0.5),
        'w_down': nrm(ks[31], (DEPTH, D_FF, D_MODEL), D_FF ** -0.5),
        'final_norm_g': 1.0 + nrm(ks[32], (D_MODEL,), 0.02),
    }


def reference(x_prompt, x_sample, cache_k, cache_v, page_table, state_mlstm_c, state_mlstm_n,
              state_mlstm_m, state_conv, state_rglru_h, norm_mix_g, w_in, w_out, lam_q1, lam_k1,
              lam_q2, lam_k2, attn_subln_g, b_ig, b_fg, mlstm_norm_g, conv_w, conv_b, w_ra, b_ra,
              w_rx, b_rx, rg_lambda, norm_mlp_g, w_up, w_down, final_norm_g):
    Bp, Sp, _ = x_prompt.shape
    Bd, Td, _ = x_sample.shape
    past = page_table.shape[1] * PAGE_SIZE
    pos_p = jnp.arange(Sp)
    pos_s = past + jnp.arange(Td)
    xp, xs = x_prompt, x_sample
    kps, vps, kss, vss = [], [], [], []
    cps, nps, mps, css, nss, mss = [], [], [], [], [], []
    cvp, cvs, hp, hs = [], [], [], []
    for l in range(DEPTH):
        p = {'norm_mix_g': norm_mix_g[l], 'w_in': w_in[l], 'w_out': w_out[l],
             'attn_subln_g': attn_subln_g[l], 'b_ig': b_ig[l], 'b_fg': b_fg[l],
             'mlstm_norm_g': mlstm_norm_g[l], 'conv_w': conv_w[l], 'conv_b': conv_b[l],
             'w_ra': w_ra[l], 'b_ra': b_ra[l], 'w_rx': w_rx[l], 'b_rx': b_rx[l],
             'rg_lambda': rg_lambda[l], 'norm_mlp_g': norm_mlp_g[l], 'w_up': w_up[l],
             'w_down': w_down[l]}
        lam_init = 0.8 - 0.6 * math.exp(-0.3 * l)
        lam = (jnp.exp(jnp.sum(lam_q1[l].astype(jnp.float32) * lam_k1[l].astype(jnp.float32)))
               - jnp.exp(jnp.sum(lam_q2[l].astype(jnp.float32) * lam_k2[l].astype(jnp.float32)))
               + lam_init)
        zc = jnp.zeros((Bp, CONV_W - 1, D_RG), xp.dtype)
        zh = jnp.zeros((Bp, D_RG), jnp.float32)
        xp, kp, vp, mst_p, cv_p, h_p = _layer(xp, pos_p, p, lam, lam_init, _diff_attn_prompt,
                                              _mlstm_chunked, zc, zh)
        k_past = cache_k[l][page_table].reshape(Bd, past, H_DA, 2, DH_DA)
        v_past = cache_v[l][page_table].reshape(Bd, past, H_DA, 2 * DH_DA)
        attn_s = functools.partial(_diff_attn_sample, k_past=k_past, v_past=v_past)
        mlstm_s = functools.partial(_mlstm_step, state=(state_mlstm_c[l], state_mlstm_n[l], state_mlstm_m[l]))
        xs, ksn, vsn, mst_s, cv_s, h_s = _layer(xs, pos_s, p, lam, lam_init, attn_s, mlstm_s,
                                                state_conv[l], state_rglru_h[l])
        kps.append(kp); vps.append(vp); kss.append(ksn); vss.append(vsn)
        cps.append(mst_p[0]); nps.append(mst_p[1]); mps.append(mst_p[2])
        css.append(mst_s[0]); nss.append(mst_s[1]); mss.append(mst_s[2])
        cvp.append(cv_p); cvs.append(cv_s); hp.append(h_p); hs.append(h_s)
    y_prompt = _rmsnorm(xp, final_norm_g, NORM_EPS)
    y_sample = _rmsnorm(xs, final_norm_g, NORM_EPS)
    return (y_prompt, y_sample, jnp.stack(kps), jnp.stack(vps), jnp.stack(kss), jnp.stack(vss),
            jnp.stack(cps), jnp.stack(nps), jnp.stack(mps), jnp.stack(css), jnp.stack(nss),
            jnp.stack(mss), jnp.stack(cvp), jnp.stack(cvs), jnp.stack(hp), jnp.stack(hs))
```

```python
import functools
import math

import jax
import jax.numpy as jnp
from jax import lax
from jax.experimental import pallas as pl
from jax.experimental.pallas import tpu as pltpu

D_MODEL = 4096
DEPTH = 2
PAGE_SIZE = 128
D_ATTN = D_MODEL // 2
D_MLSTM = D_MODEL // 4
D_RG = D_MODEL - D_ATTN - D_MLSTM
DH_DA = 128
H_DA = D_ATTN // (2 * DH_DA)
H_M = 4
DH_M = D_MLSTM // H_M
RG_BLOCKS = 8
RG_BW = D_RG // RG_BLOCKS
CONV_W = 4
RG_C = 8.0
D_FF = 4 * D_MODEL
ROPE_THETA = 10000.0
MLSTM_CHUNK = 128
NORM_EPS = 1e-6
SUBLN_EPS = 1e-5
NEG = -1e30

LANES = 128
SUBLANES = 8
VMEM_LIMIT = 56 * 1024 * 1024

F32 = jnp.float32
BF16 = jnp.bfloat16


def _params(sem):
    return pltpu.CompilerParams(dimension_semantics=sem, vmem_limit_bytes=VMEM_LIMIT)


def _sigmoid(x):
    return 1.0 / (1.0 + jnp.exp(-x))


def _softplus(x):
    return jnp.maximum(x, 0.0) + jnp.log1p(jnp.exp(-jnp.abs(x)))


def _log_sigmoid(x):
    return -_softplus(-x)


def _gelu_tanh(x):
    c = math.sqrt(2.0 / math.pi)
    return 0.5 * x * (1.0 + jnp.tanh(c * (x + 0.044715 * (x * x * x))))


def _lambda(lq1, lk1, lq2, lk2, lam_init):
    a = jnp.exp(jnp.sum(lq1[...] * lk1[...], axis=-1, keepdims=True))
    b = jnp.exp(jnp.sum(lq2[...] * lk2[...], axis=-1, keepdims=True))
    return a - b + lam_init


def _rmsnorm_kernel(x_ref, g_ref, o_ref, *, eps):
    x = x_ref[...].astype(F32)
    ms = jnp.mean(x * x, axis=-1, keepdims=True)
    o_ref[...] = (x * lax.rsqrt(ms + eps) * g_ref[...]).astype(o_ref.dtype)


def _rmsnorm(x, g, eps, out_dtype, tm):
    m, d = x.shape
    return pl.pallas_call(
        functools.partial(_rmsnorm_kernel, eps=eps),
        grid=(m // tm,),
        in_specs=[pl.BlockSpec((tm, d), lambda i: (i, 0)),
                  pl.BlockSpec((1, d), lambda i: (0, 0))],
        out_specs=pl.BlockSpec((tm, d), lambda i: (i, 0)),
        out_shape=jax.ShapeDtypeStruct((m, d), out_dtype),
        compiler_params=_params(("parallel",)),
        name="rmsnorm",
    )(x, g.reshape(1, d).astype(F32))


def _mm_kernel(*refs, nk, n_out, has_res, has_rope, act, tn):
    it = iter(refs)
    x_ref = next(it)
    w_ref = next(it)
    cos_ref = next(it) if has_rope else None
    sin_ref = next(it) if has_rope else None
    res_ref = next(it) if has_res else None
    o_refs = [next(it) for _ in range(n_out)]
    acc_ref = next(it) if nk > 1 else None

    part = jnp.dot(x_ref[...].astype(BF16), w_ref[...], preferred_element_type=F32)

    def epilogue(acc):
        if has_rope:
            cos = cos_ref[...]
            sin = sin_ref[...]
            for g in range(tn // DH_DA):
                sl = slice(g * DH_DA, (g + 1) * DH_DA)
                xg = acc[:, sl]
                yg = xg * cos + pltpu.roll(xg, DH_DA // 2, axis=1) * sin
                for o in o_refs:
                    o[:, sl] = yg.astype(o.dtype)
            return
        if act == "relu2":
            r = jnp.maximum(acc, 0.0)
            acc = r * r
        if has_res:
            acc = res_ref[...] + acc
        for o in o_refs:
            o[...] = acc.astype(o.dtype)

    if nk == 1:
        epilogue(part)
    else:
        k = pl.program_id(2)

        @pl.when(k == 0)
        def _():
            acc_ref[...] = part

        @pl.when(k > 0)
        def _():
            acc_ref[...] += part

        @pl.when(k == nk - 1)
        def _():
            epilogue(acc_ref[...])


def _matmul(x, w, *, tm, tn, tk, out_dtypes, residual=None, rope=None, act=None, name="matmul"):
    m, kdim = x.shape
    n = w.shape[1]
    nk = kdim // tk
    in_specs = [pl.BlockSpec((tm, tk), lambda i, j, k: (i, k)),
                pl.BlockSpec((tk, tn), lambda i, j, k: (k, j))]
    args = [x, w]
    if rope is not None:
        nr = rope[0].shape[0] // tm
        spec = pl.BlockSpec((tm, DH_DA), lambda i, j, k: (i % nr, 0))
        in_specs += [spec, spec]
        args += [rope[0], rope[1]]
    if residual is not None:
        in_specs.append(pl.BlockSpec((tm, tn), lambda i, j, k: (i, j)))
        args.append(residual)
    out_spec = pl.BlockSpec((tm, tn), lambda i, j, k: (i, j))
    outs = pl.pallas_call(
        functools.partial(_mm_kernel, nk=nk, n_out=len(out_dtypes), has_res=residual is not None,
                          has_rope=rope is not None, act=act, tn=tn),
        grid=(m // tm, n // tn, nk),
        in_specs=in_specs,
        out_specs=[out_spec] * len(out_dtypes),
        out_shape=[jax.ShapeDtypeStruct((m, n), dt) for dt in out_dtypes],
        scratch_shapes=[pltpu.VMEM((tm, tn), F32)] if nk > 1 else [],
        compiler_params=_params(("parallel", "parallel", "arbitrary")),
        name=name,
    )(*args)
    return outs[0] if len(out_dtypes) == 1 else outs


def _attn_prompt_kernel(lq1, lk1, lq2, lk2, g_ref, q_ref, k_ref, v_ref, o_ref,
                        m_sc, l_sc, acc_sc, *, tq, tk, lam_init):
    qi = pl.program_id(2)
    scale = DH_DA ** -0.5
    q = q_ref[...]
    row = qi * tq + lax.broadcasted_iota(jnp.int32, (tq, tk), 0)
    col = lax.broadcasted_iota(jnp.int32, (tq, tk), 1)
    m_sc[...] = jnp.full(m_sc.shape, NEG, F32)
    l_sc[...] = jnp.zeros(l_sc.shape, F32)
    acc_sc[...] = jnp.zeros(acc_sc.shape, F32)

    def body(j, carry):
        start = pl.multiple_of(j * tk, tk)
        kj = k_ref[pl.ds(start, tk), :]
        vj = v_ref[pl.ds(start, tk), :]
        mask = (col + j * tk) <= row
        for c in range(2):
            sl = slice(c * DH_DA, (c + 1) * DH_DA)
            s = lax.dot_general(q[:, sl], kj[:, sl], (((1,), (1,)), ((), ())),
                                preferred_element_type=F32) * scale
            s = jnp.where(mask, s, NEG)
            m_prev = m_sc[c]
            m_new = jnp.maximum(m_prev, jnp.max(s, axis=1, keepdims=True))
            p = jnp.exp(s - m_new)
            alpha = jnp.exp(m_prev - m_new)
            l_sc[c] = alpha * l_sc[c] + jnp.sum(p, axis=1, keepdims=True)
            acc_sc[c] = alpha * acc_sc[c] + jnp.dot(p.astype(BF16), vj, preferred_element_type=F32)
            m_sc[c] = m_new
        return carry

    n_chunks = (qi * tq + tq + tk - 1) // tk
    lax.fori_loop(0, n_chunks, body, 0)

    lam = _lambda(lq1, lk1, lq2, lk2, lam_init)
    out = acc_sc[0] / l_sc[0] - lam * (acc_sc[1] / l_sc[1])
    ms = jnp.mean(out * out, axis=-1, keepdims=True)
    y = out * lax.rsqrt(ms + SUBLN_EPS) * g_ref[...]
    o_ref[...] = (y * (1.0 - lam_init)).astype(o_ref.dtype)


def _attn_prompt(q, k, v, lam_rows, subln_g, lam_init, batch, seq, tq=512, tk=512):
    nq = seq // tq
    dh2 = 2 * DH_DA
    row_spec = pl.BlockSpec((1, DH_DA), lambda b, h, i: (0, 0))
    return pl.pallas_call(
        functools.partial(_attn_prompt_kernel, tq=tq, tk=tk, lam_init=lam_init),
        grid=(batch, H_DA, nq),
        in_specs=[row_spec, row_spec, row_spec, row_spec,
                  pl.BlockSpec((1, dh2), lambda b, h, i: (0, 0)),
                  pl.BlockSpec((tq, dh2), lambda b, h, i: (b * nq + i, h)),
                  pl.BlockSpec((seq, dh2), lambda b, h, i: (b, h)),
                  pl.BlockSpec((seq, dh2), lambda b, h, i: (b, h))],
        out_specs=pl.BlockSpec((tq, dh2), lambda b, h, i: (b * nq + i, h)),
        out_shape=jax.ShapeDtypeStruct((batch * seq, D_ATTN), BF16),
        scratch_shapes=[pltpu.VMEM((2, tq, 1), F32), pltpu.VMEM((2, tq, 1), F32),
                        pltpu.VMEM((2, tq, dh2), F32)],
        compiler_params=_params(("parallel", "parallel", "arbitrary")),
        name="attn_prompt",
    )(*lam_rows, subln_g.reshape(1, dh2), q, k, v)


def _attn_sample_kernel(pt_ref, lq1, lk1, lq2, lk2, g_ref, q_ref, kn_ref, vn_ref, k_ref, v_ref,
                        o_ref, m_sc, l_sc, acc_sc, *, lam_init, n_pages):
    del pt_ref
    p = pl.program_id(1)
    scale = DH_DA ** -0.5

    @pl.when(p == 0)
    def _():
        m_sc[...] = jnp.full(m_sc.shape, NEG, F32)
        l_sc[...] = jnp.zeros(l_sc.shape, F32)
        acc_sc[...] = jnp.zeros(acc_sc.shape, F32)

    q = q_ref[...]
    k = k_ref[...]
    v = v_ref[...]
    prod = (k * q[None]).reshape(PAGE_SIZE * H_DA, 2 * DH_DA)
    ones = jnp.ones((DH_DA, LANES), BF16)
    for c in range(2):
        sl = slice(c * DH_DA, (c + 1) * DH_DA)
        s = jnp.dot(prod[:, sl].astype(BF16), ones, preferred_element_type=F32) * scale
        s = s.reshape(PAGE_SIZE, H_DA, LANES)
        m_prev = m_sc[c]
        m_new = jnp.maximum(m_prev, jnp.max(s, axis=0))
        pe = jnp.exp(s - m_new[None])
        alpha = jnp.exp(m_prev - m_new)
        l_sc[c] = alpha * l_sc[c] + jnp.sum(pe, axis=0)
        for e in range(2):
            se = slice(e * LANES, (e + 1) * LANES)
            acc_sc[c, :, se] = alpha * acc_sc[c, :, se] + jnp.sum(pe * v[:, :, se], axis=0)
        m_sc[c] = m_new

    @pl.when(p == n_pages - 1)
    def _():
        kn = kn_ref[...]
        vn = vn_ref[...]
        pn = kn * q
        outs = []
        for c in range(2):
            sl = slice(c * DH_DA, (c + 1) * DH_DA)
            sn = jnp.sum(pn[:, sl], axis=-1, keepdims=True) * scale
            m_prev = m_sc[c]
            m_new = jnp.maximum(m_prev, sn)
            pe = jnp.exp(sn - m_new)
            alpha = jnp.exp(m_prev - m_new)
            l = alpha * l_sc[c] + pe
            halves = []
            for e in range(2):
                se = slice(e * LANES, (e + 1) * LANES)
                halves.append((alpha * acc_sc[c, :, se] + pe * vn[:, se]) / l)
            outs.append(jnp.concatenate(halves, axis=-1))
        lam = _lambda(lq1, lk1, lq2, lk2, lam_init)
        out = outs[0] - lam * outs[1]
        ms = jnp.mean(out * out, axis=-1, keepdims=True)
        y = out * lax.rsqrt(ms + SUBLN_EPS) * g_ref[...]
        o_ref[...] = (y * (1.0 - lam_init)).astype(o_ref.dtype)


def _attn_sample(page_table, q, k_new, v_new, cache_k, cache_v, layer, lam_rows, subln_g, lam_init):
    bd, n_pages = page_table.shape
    dh2 = 2 * DH_DA
    row_spec = pl.BlockSpec((1, DH_DA), lambda b, p, pt: (0, 0))
    tok_spec = pl.BlockSpec((None, H_DA, dh2), lambda b, p, pt: (b, 0, 0))
    page_spec = pl.BlockSpec((None, None, PAGE_SIZE, H_DA, dh2),
                             lambda b, p, pt: (layer, pt[b, p], 0, 0, 0))
    grid_spec = pltpu.PrefetchScalarGridSpec(
        num_scalar_prefetch=1,
        grid=(bd, n_pages),
        in_specs=[row_spec, row_spec, row_spec, row_spec,
                  pl.BlockSpec((1, dh2), lambda b, p, pt: (0, 0)),
                  tok_spec, tok_spec, tok_spec, page_spec, page_spec],
        out_specs=tok_spec,
        scratch_shapes=[pltpu.VMEM((2, H_DA, LANES), F32), pltpu.VMEM((2, H_DA, LANES), F32),
                        pltpu.VMEM((2, H_DA, dh2), F32)],
    )
    return pl.pallas_call(
        functools.partial(_attn_sample_kernel, lam_init=lam_init, n_pages=n_pages),
        grid_spec=grid_spec,
        out_shape=jax.ShapeDtypeStruct((bd, H_DA, dh2), F32),
        compiler_params=_params(("parallel", "arbitrary")),
        name="attn_sample",
    )(page_table, *lam_rows, subln_g.reshape(1, dh2), q, k_new, v_new, cache_k, cache_v)


def _mlstm_out(h, g, om):
    mu = jnp.mean(h, axis=-1, keepdims=True)
    hc = h - mu
    var = jnp.mean(hc * hc, axis=-1, keepdims=True)
    return hc * lax.rsqrt(var + NORM_EPS) * g * _sigmoid(om)


def _mlstm_prompt_kernel(big_ref, bfg_ref, q_ref, k_ref, v_ref, om_ref, gc_ref, gr_ref, ng_ref,
                         hm_ref, c_out, n_out, m_out, c_sc, n_sc, m_sc, *, n_chunks):
    ci = pl.program_id(1)
    L = MLSTM_CHUNK

    @pl.when(ci == 0)
    def _():
        c_sc[...] = jnp.zeros(c_sc.shape, F32)
        n_sc[...] = jnp.zeros(n_sc.shape, F32)
        m_sc[...] = jnp.zeros(m_sc.shape, F32)

    ri = lax.broadcasted_iota(jnp.int32, (L, L), 0)
    cj = lax.broadcasted_iota(jnp.int32, (L, L), 1)
    lower = ri >= cj
    gc = gc_ref[...]
    gr = gr_ref[...]
    for h in range(H_M):
        hs = slice(h * DH_M, (h + 1) * DH_M)
        ig_c = gc[:, h:h + 1] + big_ref[h]
        lf_c = _log_sigmoid(gc[:, H_M + h:H_M + h + 1] + bfg_ref[h])
        ig_r = gr[h:h + 1, :] + big_ref[h]
        lf_r = _log_sigmoid(gr[H_M + h:H_M + h + 1, :] + bfg_ref[h])
        b_col = jnp.sum(jnp.where(lower, lf_r, 0.0), axis=1, keepdims=True)
        b_row = jnp.sum(jnp.where(cj >= ri, lf_c, 0.0), axis=0, keepdims=True)
        dm = jnp.where(lower, b_col - b_row + ig_r, -jnp.inf)
        m_prev = m_sc[h:h + 1, 0:1]
        inter = b_col + m_prev
        m_t = jnp.maximum(inter, jnp.max(dm, axis=1, keepdims=True))
        w_intra = jnp.exp(dm - m_t)
        w_inter = jnp.exp(inter - m_t)
        q = q_ref[:, hs]
        ks = k_ref[:, hs] * (DH_M ** -0.5)
        v = v_ref[:, hs]
        c_prev = c_sc[h]
        n_prev = n_sc[h:h + 1, :]
        sc = w_intra * lax.dot_general(q, ks, (((1,), (1,)), ((), ())), preferred_element_type=F32)
        num = (jnp.dot(sc.astype(BF16), v, preferred_element_type=F32)
               + w_inter * lax.dot_general(q, c_prev.astype(BF16), (((1,), (1,)), ((), ())),
                                           preferred_element_type=F32))
        den = (jnp.sum(sc, axis=1, keepdims=True)
               + w_inter * jnp.sum(q.astype(F32) * n_prev, axis=1, keepdims=True))
        hh = num / jnp.maximum(jnp.abs(den), jnp.exp(-m_t))
        m_new = m_t[L - 1:L, :]
        decay = w_inter[L - 1:L, :]
        wk = jnp.exp(b_col[L - 1:L, :] - b_col + ig_c - m_new)
        wv = (wk * v.astype(F32)).astype(BF16)
        c_sc[h] = decay * c_prev + lax.dot_general(wv, ks, (((0,), (0,)), ((), ())),
                                                   preferred_element_type=F32)
        n_sc[h:h + 1, :] = decay * n_prev + jnp.sum(wk * ks.astype(F32), axis=0, keepdims=True)
        m_sc[h:h + 1, :] = jnp.broadcast_to(m_new, (1, LANES))
        hm_ref[:, hs] = _mlstm_out(hh, ng_ref[:, hs], om_ref[:, hs].astype(F32)).astype(hm_ref.dtype)

    @pl.when(ci == n_chunks - 1)
    def _():
        c_out[...] = c_sc[...]
        n_out[...] = n_sc[0:H_M, :]
        m_out[...] = m_sc[0:H_M, :]


def _mlstm_prompt(um, gates, gates_t, b_ig, b_fg, norm_g, batch, seq):
    L = MLSTM_CHUNK
    nc = seq // L
    smem = pl.BlockSpec(memory_space=pltpu.SMEM)

    def col(cb):
        return pl.BlockSpec((L, D_MLSTM), lambda b, c: (b * nc + c, cb))

    return pl.pallas_call(
        functools.partial(_mlstm_prompt_kernel, n_chunks=nc),
        grid=(batch, nc),
        in_specs=[smem, smem, col(0), col(1), col(2), col(3),
                  pl.BlockSpec((L, LANES), lambda b, c: (b * nc + c, 0)),
                  pl.BlockSpec((None, SUBLANES, L), lambda b, c: (b, 0, c)),
                  pl.BlockSpec((1, D_MLSTM), lambda b, c: (0, 0))],
        out_specs=[pl.BlockSpec((L, D_MLSTM), lambda b, c: (b * nc + c, 0)),
                   pl.BlockSpec((None, H_M, DH_M, DH_M), lambda b, c: (b, 0, 0, 0)),
                   pl.BlockSpec((None, H_M, DH_M), lambda b, c: (b, 0, 0)),
                   pl.BlockSpec((None, H_M, LANES), lambda b, c: (b, 0, 0))],
        out_shape=[jax.ShapeDtypeStruct((batch * seq, D_MLSTM), BF16),
                   jax.ShapeDtypeStruct((batch, H_M, DH_M, DH_M), F32),
                   jax.ShapeDtypeStruct((batch, H_M, DH_M), F32),
                   jax.ShapeDtypeStruct((batch, H_M, LANES), F32)],
        scratch_shapes=[pltpu.VMEM((H_M, DH_M, DH_M), F32), pltpu.VMEM((SUBLANES, DH_M), F32),
                        pltpu.VMEM((SUBLANES, LANES), F32)],
        compiler_params=_params(("parallel", "arbitrary")),
        name="mlstm_prompt",
    )(b_ig, b_fg, um, um, um, um, gates, gates_t, norm_g.reshape(1, D_MLSTM))


def _mlstm_sample_kernel(big_ref, bfg_ref, u_ref, vcol_ref, g_ref, ng_ref, c_ref, n_ref, m_ref,
                         hm_ref, c_out, n_out, m_out):
    u = u_ref[...]
    g = g_ref[...]
    for h in range(H_M):
        def part(i):
            return u[:, i * D_MLSTM + h * DH_M:i * D_MLSTM + (h + 1) * DH_M]
        q, k, v, om = part(0), part(1), part(2), part(3)
        ks = k * (DH_M ** -0.5)
        ig = g[:, h:h + 1] + big_ref[h]
        lf = _log_sigmoid(g[:, H_M + h:H_M + h + 1] + bfg_ref[h])
        m_prev = m_ref[:, h:h + 1]
        inter = lf + m_prev
        m_t = jnp.maximum(inter, ig)
        w_intra = jnp.exp(ig - m_t)
        w_inter = jnp.exp(inter - m_t)
        c_prev = c_ref[h]
        n_prev = n_ref[h:h + 1, :]
        q8 = jnp.broadcast_to(q, (2 * SUBLANES, DH_M)).astype(BF16)
        cq = lax.dot_general(q8, c_prev.astype(BF16), (((1,), (1,)), ((), ())),
                             preferred_element_type=F32)[0:1, :]
        sc = w_intra * jnp.sum(q * ks, axis=-1, keepdims=True)
        num = sc * v + w_inter * cq
        den = sc + w_inter * jnp.sum(n_prev * q, axis=-1, keepdims=True)
        hh = num / jnp.maximum(jnp.abs(den), jnp.exp(-m_t))
        c_out[h] = w_inter * c_prev + (w_intra * vcol_ref[h]) * ks
        n_out[h:h + 1, :] = w_inter * n_prev + w_intra * ks
        m_out[h:h + 1, :] = jnp.broadcast_to(m_t, (1, LANES))
        hs = slice(h * DH_M, (h + 1) * DH_M)
        hm_ref[:, hs] = _mlstm_out(hh, ng_ref[:, hs], om)


def _mlstm_sample(um, gates, b_ig, b_fg, norm_g, c_state, n_state, m_state):
    bd = um.shape[0]
    smem = pl.BlockSpec(memory_space=pltpu.SMEM)
    u4 = um[:, :4 * D_MLSTM].reshape(bd, 1, 4 * D_MLSTM)
    vcol = um[:, 2 * D_MLSTM:3 * D_MLSTM].reshape(bd, H_M, DH_M, 1)
    return pl.pallas_call(
        _mlstm_sample_kernel,
        grid=(bd,),
        in_specs=[smem, smem,
                  pl.BlockSpec((None, 1, 4 * D_MLSTM), lambda b: (b, 0, 0)),
                  pl.BlockSpec((None, H_M, DH_M, 1), lambda b: (b, 0, 0, 0)),
                  pl.BlockSpec((None, 1, LANES), lambda b: (b, 0, 0)),
                  pl.BlockSpec((1, D_MLSTM), lambda b: (0, 0)),
                  pl.BlockSpec((None, H_M, DH_M, DH_M), lambda b: (b, 0, 0, 0)),
                  pl.BlockSpec((None, H_M, DH_M), lambda b: (b, 0, 0)),
                  pl.BlockSpec((None, 1, H_M), lambda b: (b, 0, 0))],
        out_specs=[pl.BlockSpec((None, 1, D_MLSTM), lambda b: (b, 0, 0)),
                   pl.BlockSpec((None, H_M, DH_M, DH_M), lambda b: (b, 0, 0, 0)),
                   pl.BlockSpec((None, H_M, DH_M), lambda b: (b, 0, 0)),
                   pl.BlockSpec((None, H_M, LANES), lambda b: (b, 0, 0))],
        out_shape=[jax.ShapeDtypeStruct((bd, 1, D_MLSTM), F32),
                   jax.ShapeDtypeStruct((bd, H_M, DH_M, DH_M), F32),
                   jax.ShapeDtypeStruct((bd, H_M, DH_M), F32),
                   jax.ShapeDtypeStruct((bd, H_M, LANES), F32)],
        compiler_params=_params(("parallel",)),
        name="mlstm_sample",
    )(b_ig, b_fg, u4, vcol, gates.reshape(bd, 1, LANES), norm_g.reshape(1, D_MLSTM),
      c_state, n_state, m_state.reshape(bd, 1, H_M))


def _rglru_coeffs(xc, wra_ref, bra_ref, wrx_ref, brx_ref, lam_ref):
    rs, is_ = [], []
    for n in range(RG_BLOCKS):
        xb = xc[:, n * RG_BW:(n + 1) * RG_BW].astype(BF16)
        rs.append(jnp.dot(xb, wra_ref[n], preferred_element_type=F32))
        is_.append(jnp.dot(xb, wrx_ref[n], preferred_element_type=F32))
    r = _sigmoid(jnp.concatenate(rs, axis=-1) + bra_ref[...])
    i = _sigmoid(jnp.concatenate(is_, axis=-1) + brx_ref[...])
    log_a = -RG_C * r * _softplus(-lam_ref[...])
    a = jnp.exp(log_a)
    mult = jnp.sqrt(1.0 - a * a)
    return a, mult * i * xc


def _rglru_prompt_kernel(x_ref, gate_ref, cw_ref, cb_ref, wra_ref, bra_ref, wrx_ref, brx_ref, lam_ref,
                         y_ref, conv_out, h_out, xbuf, a_sc, b_sc, h_sc, *, tt, n_tiles):
    ti = pl.program_id(1)
    pad = SUBLANES

    @pl.when(ti == 0)
    def _():
        xbuf[0:pad, :] = jnp.zeros((pad, D_RG), F32)
        h_sc[...] = jnp.zeros(h_sc.shape, F32)

    xbuf[pad:pad + tt, :] = x_ref[...].astype(F32)
    xc = cb_ref[...] + sum(xbuf[pad - (CONV_W - 1) + j:pad - (CONV_W - 1) + j + tt, :] * cw_ref[j:j + 1, :]
                           for j in range(CONV_W))
    a, b = _rglru_coeffs(xc, wra_ref, bra_ref, wrx_ref, brx_ref, lam_ref)
    a_sc[...] = a
    b_sc[...] = b

    def step(t, h):
        h = a_sc[pl.ds(t, 1), :] * h + b_sc[pl.ds(t, 1), :]
        b_sc[pl.ds(t, 1), :] = h
        return h

    h_last = lax.fori_loop(0, tt, step, h_sc[...], unroll=8)
    h_sc[...] = h_last
    y_ref[...] = (b_sc[...] * _gelu_tanh(gate_ref[...].astype(F32))).astype(y_ref.dtype)
    tail = xbuf[tt:tt + pad, :]
    xbuf[0:pad, :] = tail

    @pl.when(ti == n_tiles - 1)
    def _():
        conv_out[...] = tail[pad - (CONV_W - 1):, :]
        h_out[...] = h_last


def _rglru_weights(conv_w, conv_b, w_ra, b_ra, w_rx, b_rx, lam):
    row = lambda z: z.reshape(1, D_RG).astype(F32)
    return (conv_w.astype(F32), row(conv_b), w_ra.astype(BF16), row(b_ra), w_rx.astype(BF16), row(b_rx), row(lam))


def _rglru_weight_specs(nidx):
    zeros2 = (lambda *a: (0, 0))
    zeros3 = (lambda *a: (0, 0, 0))
    del nidx
    return [pl.BlockSpec((CONV_W, D_RG), zeros2), pl.BlockSpec((1, D_RG), zeros2),
            pl.BlockSpec((RG_BLOCKS, RG_BW, RG_BW), zeros3), pl.BlockSpec((1, D_RG), zeros2),
            pl.BlockSpec((RG_BLOCKS, RG_BW, RG_BW), zeros3), pl.BlockSpec((1, D_RG), zeros2),
            pl.BlockSpec((1, D_RG), zeros2)]


def _rglru_prompt(um, weights, batch, seq, tt=256):
    nt = seq // tt
    xcol = 4 * D_MLSTM // D_RG
    return pl.pallas_call(
        functools.partial(_rglru_prompt_kernel, tt=tt, n_tiles=nt),
        grid=(batch, nt),
        in_specs=[pl.BlockSpec((tt, D_RG), lambda b, t: (b * nt + t, xcol)),
                  pl.BlockSpec((tt, D_RG), lambda b, t: (b * nt + t, xcol + 1))] + _rglru_weight_specs(2),
        out_specs=[pl.BlockSpec((tt, D_RG), lambda b, t: (b * nt + t, 0)),
                   pl.BlockSpec((None, CONV_W - 1, D_RG), lambda b, t: (b, 0, 0)),
                   pl.BlockSpec((None, 1, D_RG), lambda b, t: (b, 0, 0))],
        out_shape=[jax.ShapeDtypeStruct((batch * seq, D_RG), BF16),
                   jax.ShapeDtypeStruct((batch, CONV_W - 1, D_RG), F32),
                   jax.ShapeDtypeStruct((batch, 1, D_RG), F32)],
        scratch_shapes=[pltpu.VMEM((tt + SUBLANES, D_RG), F32), pltpu.VMEM((tt, D_RG), F32),
                        pltpu.VMEM((tt, D_RG), F32), pltpu.VMEM((1, D_RG), F32)],
        compiler_params=_params(("parallel", "arbitrary")),
        name="rglru_prompt",
    )(um, um, *weights)


def _rglru_sample_kernel(x_ref, gate_ref, p0_ref, p1_ref, p2_ref, h0_ref, cw_ref, cb_ref, wra_ref, bra_ref,
                         wrx_ref, brx_ref, lam_ref, y_ref, h_out):
    x = x_ref[...]
    xc = cb_ref[...] + (p0_ref[...] * cw_ref[0:1, :] + p1_ref[...] * cw_ref[1:2, :]
                        + p2_ref[...] * cw_ref[2:3, :] + x * cw_ref[3:4, :])
    a, b = _rglru_coeffs(xc, wra_ref, bra_ref, wrx_ref, brx_ref, lam_ref)
    h = a * h0_ref[...] + b
    h_out[...] = h
    y_ref[...] = h * _gelu_tanh(gate_ref[...])


def _rglru_sample(x, gate, conv_prev, h0, weights):
    bd = x.shape[0]
    full = pl.BlockSpec((bd, D_RG), lambda i: (0, 0))
    return pl.pallas_call(
        _rglru_sample_kernel,
        grid=(1,),
        in_specs=[full] * 6 + _rglru_weight_specs(1),
        out_specs=[full, full],
        out_shape=[jax.ShapeDtypeStruct((bd, D_RG), F32)] * 2,
        compiler_params=_params(("arbitrary",)),
        name="rglru_sample",
    )(x, gate, conv_prev[:, 0], conv_prev[:, 1], conv_prev[:, 2], h0, *weights)


def _rope_tables(pos):
    inv = 1.0 / (ROPE_THETA ** (jnp.arange(0, DH_DA, 2, dtype=F32) / DH_DA))
    ang = pos.astype(F32)[:, None] * inv[None, :]
    ang = jnp.concatenate([ang, ang], axis=-1)
    sign = jnp.concatenate([-jnp.ones((DH_DA // 2,), F32), jnp.ones((DH_DA // 2,), F32)])
    return jnp.cos(ang), jnp.sin(ang) * sign[None, :]


def _split_w_in(w):
    a = 3 * D_ATTN
    g0 = a + 4 * D_MLSTM
    g1 = g0 + 2 * H_M
    wq = w[:, :D_ATTN].astype(BF16)
    wk = w[:, D_ATTN:2 * D_ATTN].astype(BF16)
    wv = w[:, 2 * D_ATTN:a].astype(BF16)
    wm = jnp.concatenate([w[:, a:g0], w[:, g1:]], axis=1).astype(BF16)
    wg = jnp.pad(w[:, g0:g1], ((0, 0), (0, LANES - 2 * H_M))).astype(BF16)
    return wq, wk, wv, wm, wg


def kernel(x_prompt, x_sample, cache_k, cache_v, page_table, state_mlstm_c, state_mlstm_n, state_mlstm_m, state_conv, state_rglru_h, norm_mix_g, w_in, w_out, lam_q1, lam_k1, lam_q2, lam_k2, attn_subln_g, b_ig, b_fg, mlstm_norm_g, conv_w, conv_b, w_ra, b_ra, w_rx, b_rx, rg_lambda, norm_mlp_g, w_up, w_down, final_norm_g):
    bp, sp, _ = x_prompt.shape
    bd, td, _ = x_sample.shape
    assert td == 1
    past = page_table.shape[1] * PAGE_SIZE
    mp = bp * sp
    cos_p, sin_p = _rope_tables(jnp.arange(sp))
    cos_s, sin_s = _rope_tables(jnp.full((bd,), past))

    TM, TN = 1024, 512
    TNS = 1024

    xp = x_prompt.reshape(mp, D_MODEL)
    xs = x_sample.reshape(bd, D_MODEL)
    outs = {n: [] for n in ("kp", "vp", "ks", "vs", "cp", "np", "mp", "cs", "ns", "ms",
                            "cvp", "cvs", "hp", "hs")}
    for l in range(DEPTH):
        lam_init = 0.8 - 0.6 * math.exp(-0.3 * l)
        lam_rows = [z[l].reshape(1, DH_DA).astype(F32) for z in (lam_q1, lam_k1, lam_q2, lam_k2)]
        wq, wk, wv, wm, wg = _split_w_in(w_in[l])
        wo = w_out[l].astype(BF16)
        wu = w_up[l].astype(BF16)
        wd = w_down[l].astype(BF16)
        rg_w = _rglru_weights(conv_w[l], conv_b[l], w_ra[l], b_ra[l], w_rx[l], b_rx[l], rg_lambda[l])

        hn = _rmsnorm(xp, norm_mix_g[l], NORM_EPS, BF16, tm=512)
        mm = functools.partial(_matmul, hn, tm=TM, tk=D_MODEL)
        q = mm(wq, tn=TN, out_dtypes=(BF16,), rope=(cos_p, sin_p), name="proj_q")
        k32, k16 = mm(wk, tn=TN, out_dtypes=(F32, BF16), rope=(cos_p, sin_p), name="proj_k")
        v32, v16 = mm(wv, tn=TN, out_dtypes=(F32, BF16), name="proj_v")
        um = mm(wm, tn=TN, out_dtypes=(BF16,), name="proj_m")
        gates = mm(wg, tn=LANES, out_dtypes=(F32,), name="proj_g")
        gates_t = gates[:, :SUBLANES].reshape(bp, sp, SUBLANES).transpose(0, 2, 1)
        att = _attn_prompt(q, k16, v16, lam_rows, attn_subln_g[l], lam_init, bp, sp)
        hm, c_p, n_p, m_p = _mlstm_prompt(um, gates, gates_t, b_ig[l], b_fg[l], mlstm_norm_g[l], bp, sp)
        hr, cv_p, h_p = _rglru_prompt(um, rg_w, bp, sp)
        cat = jnp.concatenate([att, hm, hr], axis=1)
        x1 = _matmul(cat, wo, tm=TM, tn=TN, tk=D_MODEL, out_dtypes=(F32,), residual=xp, name="proj_out")
        hn2 = _rmsnorm(x1, norm_mlp_g[l], NORM_EPS, BF16, tm=512)
        act = _matmul(hn2, wu, tm=TM, tn=TN, tk=D_MODEL, out_dtypes=(BF16,), act="relu2", name="mlp_up")
        xp = _matmul(act, wd, tm=TM, tn=TN, tk=D_MODEL, out_dtypes=(F32,), residual=x1, name="mlp_down")

        hs_n = _rmsnorm(xs, norm_mix_g[l], NORM_EPS, F32, tm=bd)
        mms = functools.partial(_matmul, hs_n, tm=bd, tk=D_MODEL, out_dtypes=(F32,))
        q_s = mms(wq, tn=TNS, rope=(cos_s, sin_s), name="s_proj_q")
        k_s = mms(wk, tn=TNS, rope=(cos_s, sin_s), name="s_proj_k")
        v_s = mms(wv, tn=TNS, name="s_proj_v")
        um_s = mms(wm, tn=TNS, name="s_proj_m")
        gates_s = mms(wg, tn=LANES, name="s_proj_g")
        dh2 = 2 * DH_DA
        att_s = _attn_sample(page_table, q_s.reshape(bd, H_DA, dh2), k_s.reshape(bd, H_DA, dh2),
                             v_s.reshape(bd, H_DA, dh2), cache_k, cache_v, l, lam_rows,
                             attn_subln_g[l], lam_init)
        hm_s, c_s, n_s, m_s = _mlstm_sample(um_s, gates_s, b_ig[l], b_fg[l], mlstm_norm_g[l],
                                            state_mlstm_c[l], state_mlstm_n[l], state_mlstm_m[l])
        x_rg = um_s[:, 4 * D_MLSTM:4 * D_MLSTM + D_RG]
        hr_s, h_s = _rglru_sample(x_rg, um_s[:, 4 * D_MLSTM + D_RG:], state_conv[l], state_rglru_h[l], rg_w)
        cat_s = jnp.concatenate([att_s.reshape(bd, D_ATTN), hm_s.reshape(bd, D_MLSTM), hr_s], axis=1)
        x1s = _matmul(cat_s, wo, tm=bd, tn=TNS, tk=D_MODEL, out_dtypes=(F32,), residual=xs, name="s_proj_out")
        hs_n2 = _rmsnorm(x1s, norm_mlp_g[l], NORM_EPS, F32, tm=bd)
        act_s = _matmul(hs_n2, wu, tm=bd, tn=TNS, tk=D_MODEL, out_dtypes=(F32,), act="relu2", name="s_mlp_up")
        xs = _matmul(act_s, wd, tm=bd, tn=TNS, tk=D_MODEL, out_dtypes=(F32,), residual=x1s, name="s_mlp_down")

        outs["kp"].append(k32.reshape(bp, sp, H_DA, dh2))
        outs["vp"].append(v32.reshape(bp, sp, H_DA, dh2))
        outs["ks"].append(k_s.reshape(bd, td, H_DA, dh2))
        outs["vs"].append(v_s.reshape(bd, td, H_DA, dh2))
        outs["cp"].append(c_p)
        outs["np"].append(n_p)
        outs["mp"].append(m_p[:, :, 0])
        outs["cs"].append(c_s)
        outs["ns"].append(n_s)
        outs["ms"].append(m_s[:, :, 0])
        outs["cvp"].append(cv_p)
        outs["cvs"].append(jnp.concatenate([state_conv[l][:, 1:], x_rg[:, None, :]], axis=1))
        outs["hp"].append(h_p.reshape(bp, D_RG))
        outs["hs"].append(h_s)

    y_prompt = _rmsnorm(xp, final_norm_g, NORM_EPS, F32, tm=512).reshape(bp, sp, D_MODEL)
    y_sample = _rmsnorm(xs, final_norm_g, NORM_EPS, F32, tm=bd).reshape(bd, td, D_MODEL)
    st = lambda n: jnp.stack(outs[n])
    return (y_prompt, y_sample, st("kp"), st("vp"), st("ks"), st("vs"), st("cp"), st("np"), st("mp"),
            st("cs"), st("ns"), st("ms"), st("cvp"), st("cvs"), st("hp"), st("hs"))
```

```python
import functools
import math

import jax
import jax.numpy as jnp
from jax import lax
from jax.experimental import pallas as pl
from jax.experimental.pallas import tpu as pltpu

D_MODEL = 4096
DEPTH = 2
PAGE_SIZE = 128
D_ATTN = D_MODEL // 2
D_MLSTM = D_MODEL // 4
D_RG = D_MODEL - D_ATTN - D_MLSTM
DH_DA = 128
H_DA = D_ATTN // (2 * DH_DA)
H_M = 4
DH_M = D_MLSTM // H_M
RG_BLOCKS = 8
RG_BW = D_RG // RG_BLOCKS
CONV_W = 4
RG_C = 8.0
D_FF = 4 * D_MODEL
ROPE_THETA = 10000.0
MLSTM_CHUNK = 128
NORM_EPS = 1e-6
SUBLN_EPS = 1e-5
NEG = -1e30
D_MAIN = 3 * D_ATTN + 4 * D_MLSTM
ATTN_QSCALE = DH_DA ** -0.5 * math.log2(math.e)

LANES = 128
SUBLANES = 8
VMEM_LIMIT = 56 * 1024 * 1024

F32 = jnp.float32
BF16 = jnp.bfloat16


def _params(sem):
    return pltpu.CompilerParams(dimension_semantics=sem, vmem_limit_bytes=VMEM_LIMIT)


def _sigmoid(x):
    return 1.0 / (1.0 + jnp.exp(-x))


def _softplus(x):
    return jnp.maximum(x, 0.0) + jnp.log1p(jnp.exp(-jnp.abs(x)))


def _log_sigmoid(x):
    return -_softplus(-x)


def _gelu_tanh(x):
    c = math.sqrt(2.0 / math.pi)
    return 0.5 * x * (1.0 + jnp.tanh(c * (x + 0.044715 * (x * x * x))))


def _lambda(lq1, lk1, lq2, lk2, lam_init):
    a = jnp.exp(jnp.sum(lq1[...] * lk1[...], axis=-1, keepdims=True))
    b = jnp.exp(jnp.sum(lq2[...] * lk2[...], axis=-1, keepdims=True))
    return a - b + lam_init


def _rotate(x, cos, sin_signed):
    return x * cos + pltpu.roll(x, DH_DA // 2, axis=1) * sin_signed


def _rmsnorm_kernel(x_ref, g_ref, o_ref, *, eps):
    x = x_ref[...].astype(F32)
    ms = jnp.mean(x * x, axis=-1, keepdims=True)
    o_ref[...] = (x * lax.rsqrt(ms + eps) * g_ref[...]).astype(o_ref.dtype)


def _rmsnorm(x, g, eps, out_dtype, tm):
    m, d = x.shape
    return pl.pallas_call(
        functools.partial(_rmsnorm_kernel, eps=eps),
        grid=(m // tm,),
        in_specs=[pl.BlockSpec((tm, d), lambda i: (i, 0)),
                  pl.BlockSpec((1, d), lambda i: (0, 0))],
        out_specs=pl.BlockSpec((tm, d), lambda i: (i, 0)),
        out_shape=jax.ShapeDtypeStruct((m, d), out_dtype),
        compiler_params=_params(("parallel",)),
        name="rmsnorm",
    )(x, g.reshape(1, d).astype(F32))


def _mm_kernel(*refs, nk, seg, out_depths, n_prev, has_res, has_rope, rope_scale, act, tn):
    it = iter(refs)
    x_refs = [next(it) for _ in seg]
    w_ref = next(it)
    cos_ref = next(it) if has_rope else None
    sin_ref = next(it) if has_rope else None
    res_ref = next(it) if has_res else None
    prev_refs = [next(it) for _ in range(n_prev)]
    o_refs = [next(it) for _ in out_depths]
    acc_ref = next(it) if nk > 1 else None

    if len(seg) == 1:
        part = jnp.dot(x_refs[0][...].astype(BF16), w_ref[...], preferred_element_type=F32)
    else:
        part, off = None, 0
        for x_ref, width in zip(x_refs, seg):
            d = jnp.dot(x_ref[...], w_ref[off:off + width, :], preferred_element_type=F32)
            part = d if part is None else part + d
            off += width

    def store(sl, val):
        for o, depth in zip(o_refs, out_depths):
            if depth is None:
                o[:, sl] = val.astype(o.dtype)
            else:
                o[depth - 1, :, sl] = val.astype(o.dtype)

    def epilogue(acc):
        pi = 0
        for o, depth in zip(o_refs, out_depths):
            if depth is not None and depth > 1:
                o[0:depth - 1] = prev_refs[pi][...]
                pi += 1
        if has_rope:
            cos = cos_ref[...]
            sin = sin_ref[...]
            for g in range(tn // DH_DA):
                sl = slice(g * DH_DA, (g + 1) * DH_DA)
                store(sl, _rotate(acc[:, sl], cos, sin) * rope_scale)
            return
        if act == "relu2":
            r = jnp.maximum(acc, 0.0)
            acc = r * r
        if has_res:
            acc = res_ref[...] + acc
        store(slice(None), acc)

    if nk == 1:
        epilogue(part)
    else:
        k = pl.program_id(2)

        @pl.when(k == 0)
        def _():
            acc_ref[...] = part

        @pl.when(k > 0)
        def _():
            acc_ref[...] += part

        @pl.when(k == nk - 1)
        def _():
            epilogue(acc_ref[...])


def _matmul(xs, w, *, tm, tn, tk, out_dtypes, n=None, w_col0=0, residual=None, rope=None, rope_scale=1.0,
            act=None, stacked=None, prev=None, name="matmul"):
    xs = list(xs) if isinstance(xs, (list, tuple)) else [xs]
    seg = [x.shape[1] for x in xs]
    m = xs[0].shape[0]
    kdim = sum(seg)
    n = w.shape[1] - w_col0 if n is None else n
    nk = kdim // tk
    assert len(xs) == 1 or nk == 1
    jb = w_col0 // tn
    stacked = stacked or [False] * len(out_dtypes)
    prev = prev or [None] * len(out_dtypes)

    if len(xs) == 1:
        in_specs = [pl.BlockSpec((tm, tk), lambda i, j, k: (i, k))]
    else:
        in_specs = [pl.BlockSpec((tm, s), lambda i, j, k: (i, 0)) for s in seg]
    in_specs.append(pl.BlockSpec((tk, tn), lambda i, j, k: (k, j + jb)))
    args = xs + [w]
    if rope is not None:
        nr = rope[0].shape[0] // tm
        spec = pl.BlockSpec((tm, DH_DA), lambda i, j, k: (i % nr, 0))
        in_specs += [spec, spec]
        args += [rope[0], rope[1]]
    if residual is not None:
        in_specs.append(pl.BlockSpec((tm, tn), lambda i, j, k: (i, j)))
        args.append(residual)
    out_depths, out_specs, out_shapes, n_prev = [], [], [], 0
    for dt, st, pv in zip(out_dtypes, stacked, prev):
        if not st:
            out_depths.append(None)
            out_specs.append(pl.BlockSpec((tm, tn), lambda i, j, k: (i, j)))
            out_shapes.append(jax.ShapeDtypeStruct((m, n), dt))
            continue
        depth = 1 if pv is None else pv.shape[0] + 1
        out_depths.append(depth)
        out_specs.append(pl.BlockSpec((depth, tm, tn), lambda i, j, k: (0, i, j)))
        out_shapes.append(jax.ShapeDtypeStruct((depth, m, n), dt))
        if pv is not None:
            in_specs.append(pl.BlockSpec((depth - 1, tm, tn), lambda i, j, k: (0, i, j)))
            args.append(pv)
            n_prev += 1
    outs = pl.pallas_call(
        functools.partial(_mm_kernel, nk=nk, seg=seg, out_depths=out_depths, n_prev=n_prev,
                          has_res=residual is not None, has_rope=rope is not None, rope_scale=rope_scale,
                          act=act, tn=tn),
        grid=(m // tm, n // tn, nk),
        in_specs=in_specs,
        out_specs=out_specs,
        out_shape=out_shapes,
        scratch_shapes=[pltpu.VMEM((tm, tn), F32)] if nk > 1 else [],
        compiler_params=_params(("parallel", "parallel", "arbitrary")),
        name=name,
    )(*args)
    return outs[0] if len(out_dtypes) == 1 else outs


def _cast_mm_kernel(*refs, nk, has_res, act):
    it = iter(refs)
    x_ref = next(it)
    w_ref = next(it)
    res_ref = next(it) if has_res else None
    wb_ref = next(it)
    y_ref = next(it)
    k = pl.program_id(1)
    wb = w_ref[...].astype(BF16)
    wb_ref[...] = wb
    part = jnp.dot(x_ref[...].astype(BF16), wb, preferred_element_type=F32)

    @pl.when(k == 0)
    def _():
        y_ref[...] = part

    @pl.when(k > 0)
    def _():
        y_ref[...] += part

    if act is not None or has_res:
        @pl.when(k == nk - 1)
        def _():
            acc = y_ref[...]
            if act == "relu2":
                r = jnp.maximum(acc, 0.0)
                acc = r * r
            if has_res:
                acc = res_ref[...] + acc
            y_ref[...] = acc


def _cast_matmul(x, w_all, layer, n, *, tn, tk, residual=None, act=None, name="cast_matmul"):
    m, kdim = x.shape
    nk = kdim // tk
    in_specs = [pl.BlockSpec((m, tk), lambda j, k: (0, k)),
                pl.BlockSpec((None, tk, tn), lambda j, k: (layer, k, j))]
    args = [x, w_all]
    if residual is not None:
        in_specs.append(pl.BlockSpec((m, tn), lambda j, k: (0, j)))
        args.append(residual)
    wb, y = pl.pallas_call(
        functools.partial(_cast_mm_kernel, nk=nk, has_res=residual is not None, act=act),
        grid=(n // tn, nk),
        in_specs=in_specs,
        out_specs=[pl.BlockSpec((tk, tn), lambda j, k: (k, j)),
                   pl.BlockSpec((m, tn), lambda j, k: (0, j))],
        out_shape=[jax.ShapeDtypeStruct((kdim, n), BF16), jax.ShapeDtypeStruct((m, n), F32)],
        compiler_params=_params(("parallel", "arbitrary")),
        name=name,
    )(*args)
    return y, wb


def _rope_rows_kernel(x_ref, cos_ref, sin_ref, o_ref, *, n_q):
    cos = cos_ref[...]
    sin = sin_ref[...]
    for g in range(x_ref.shape[1] // DH_DA):
        sl = slice(g * DH_DA, (g + 1) * DH_DA)
        y = _rotate(x_ref[:, sl], cos, sin)
        o_ref[:, sl] = y * ATTN_QSCALE if g < n_q else y


def _rope_rows(x, cos, sin):
    m, n = x.shape
    full = lambda c: pl.BlockSpec((m, c), lambda i: (0, 0))
    return pl.pallas_call(
        functools.partial(_rope_rows_kernel, n_q=D_ATTN // DH_DA),
        grid=(1,),
        in_specs=[full(n), full(DH_DA), full(DH_DA)],
        out_specs=full(n),
        out_shape=jax.ShapeDtypeStruct((m, n), F32),
        compiler_params=_params(("arbitrary",)),
        name="rope_sample",
    )(x, cos, sin)


def _attn_prompt_kernel(lq1, lk1, lq2, lk2, g_ref, q_ref, k_ref, v_ref, o_ref, *, t, nq, lam_init):
    qi = pl.program_id(2)
    q = q_ref[...]
    lam = _lambda(lq1, lk1, lq2, lk2, lam_init)
    tri = (lax.broadcasted_iota(jnp.int32, (t, t), 1) <= lax.broadcasted_iota(jnp.int32, (t, t), 0))

    def block(n_below):
        spans = ([(0, n_below * t, False)] if n_below else []) + [(n_below * t, (n_below + 1) * t, True)]
        probs, inv = [], []
        for c in range(2):
            sl = slice(c * DH_DA, (c + 1) * DH_DA)
            ss = []
            for lo, hi, diagonal in spans:
                s = lax.dot_general(q[:, sl], k_ref[lo:hi, sl], (((1,), (1,)), ((), ())),
                                    preferred_element_type=F32)
                ss.append(jnp.where(tri, s, NEG) if diagonal else s)
            m = functools.reduce(jnp.maximum, [jnp.max(s, axis=1, keepdims=True) for s in ss])
            ps = [jnp.exp2(s - m) for s in ss]
            probs.append(ps)
            inv.append(1.0 / functools.reduce(jnp.add, [jnp.sum(p, axis=1, keepdims=True) for p in ps]))
        r1 = inv[0]
        r2 = lam * inv[1]
        out = None
        for i, (lo, hi, _) in enumerate(spans):
            a = (probs[0][i] * r1 - probs[1][i] * r2).astype(BF16)
            d = jnp.dot(a, v_ref[lo:hi, :], preferred_element_type=F32)
            out = d if out is None else out + d
        ms = jnp.mean(out * out, axis=-1, keepdims=True)
        y = out * lax.rsqrt(ms + SUBLN_EPS) * g_ref[...]
        o_ref[...] = (y * (1.0 - lam_init)).astype(o_ref.dtype)

    for n_below in range(nq):
        pl.when(qi == n_below)(functools.partial(block, n_below))


def _attn_prompt(q, k, v, lam_rows, subln_g, lam_init, batch, seq, t=512):
    nq = seq // t
    dh2 = 2 * DH_DA
    row_spec = pl.BlockSpec((1, DH_DA), lambda b, h, i: (0, 0))
    return pl.pallas_call(
        functools.partial(_attn_prompt_kernel, t=t, nq=nq, lam_init=lam_init),
        grid=(batch, H_DA, nq),
        in_specs=[row_spec, row_spec, row_spec, row_spec,
                  pl.BlockSpec((1, dh2), lambda b, h, i: (0, 0)),
                  pl.BlockSpec((t, dh2), lambda b, h, i: (b * nq + i, h)),
                  pl.BlockSpec((seq, dh2), lambda b, h, i: (b, h)),
                  pl.BlockSpec((seq, dh2), lambda b, h, i: (b, h))],
        out_specs=pl.BlockSpec((t, dh2), lambda b, h, i: (b * nq + i, h)),
        out_shape=jax.ShapeDtypeStruct((batch * seq, D_ATTN), BF16),
        compiler_params=_params(("parallel", "parallel", "arbitrary")),
        name="attn_prompt",
    )(*lam_rows, subln_g.reshape(1, dh2), q, k, v)


def _attn_sample_kernel(*refs, lam_init, n_steps, group):
    pt_ref, lq1, lk1, lq2, lk2, g_ref, q_ref, kn_ref, vn_ref = refs[:9]
    k_refs = refs[9:9 + group]
    v_refs = refs[9 + group:9 + 2 * group]
    o_ref, m_sc, l_sc, acc_sc = refs[9 + 2 * group:]
    del pt_ref
    step = pl.program_id(1)
    rows = PAGE_SIZE * H_DA

    @pl.when(step == 0)
    def _():
        m_sc[...] = jnp.full(m_sc.shape, NEG, F32)
        l_sc[...] = jnp.zeros(l_sc.shape, F32)
        acc_sc[...] = jnp.zeros(acc_sc.shape, F32)

    q = q_ref[...]
    zero = jnp.zeros((H_DA, DH_DA), F32)
    qbd = jnp.concatenate([jnp.concatenate([q[:, :DH_DA], zero], axis=1),
                           jnp.concatenate([zero, q[:, DH_DA:]], axis=1)], axis=0).astype(BF16)
    own_head = (lax.broadcasted_iota(jnp.int32, (2 * H_DA, rows), 0) % H_DA
                == lax.broadcasted_iota(jnp.int32, (2 * H_DA, rows), 1) % H_DA)
    scores = []
    for g in range(group):
        k2 = k_refs[g][...].reshape(rows, 2 * DH_DA).astype(BF16)
        s = lax.dot_general(qbd, k2, (((1,), (1,)), ((), ())), preferred_element_type=F32)
        scores.append(jnp.where(own_head, s, NEG))
    m_prev = m_sc[...]
    m_new = m_prev
    for s in scores:
        m_new = jnp.maximum(m_new, jnp.max(s, axis=1, keepdims=True))
    alpha = jnp.exp2(m_prev - m_new)
    l_new = alpha * l_sc[...]
    acc = alpha * acc_sc[...]
    for g in range(group):
        p = jnp.exp2(scores[g] - m_new)
        l_new = l_new + jnp.sum(p, axis=1, keepdims=True)
        v2 = v_refs[g][...].reshape(rows, 2 * DH_DA).astype(BF16)
        acc = acc + jnp.dot(p.astype(BF16), v2, preferred_element_type=F32)
    m_sc[...] = m_new
    l_sc[...] = l_new
    acc_sc[...] = acc

    @pl.when(step == n_steps - 1)
    def _():
        kn = kn_ref[...]
        vn = vn_ref[...]
        pn = kn * q
        sn = jnp.concatenate([jnp.sum(pn[:, :DH_DA], axis=-1, keepdims=True),
                              jnp.sum(pn[:, DH_DA:], axis=-1, keepdims=True)], axis=0)
        m_fin = jnp.maximum(m_new, sn)
        pe = jnp.exp2(sn - m_fin)
        a_fin = jnp.exp2(m_new - m_fin)
        l_fin = a_fin * l_new + pe
        outs = (a_fin * acc + pe * jnp.concatenate([vn, vn], axis=0)) / l_fin
        lam = _lambda(lq1, lk1, lq2, lk2, lam_init)
        out = outs[0:H_DA] - lam * outs[H_DA:]
        ms = jnp.mean(out * out, axis=-1, keepdims=True)
        y = out * lax.rsqrt(ms + SUBLN_EPS) * g_ref[...]
        o_ref[...] = (y * (1.0 - lam_init)).astype(o_ref.dtype)


def _attn_sample(page_table, q, k_new, v_new, cache_k, cache_v, layer, lam_rows, subln_g, lam_init, group=8):
    bd, n_pages = page_table.shape
    n_steps = n_pages // group
    dh2 = 2 * DH_DA
    row_spec = pl.BlockSpec((1, DH_DA), lambda b, p, pt: (0, 0))
    tok_spec = pl.BlockSpec((None, H_DA, dh2), lambda b, p, pt: (b, 0, 0))

    def page_spec(g):
        return pl.BlockSpec((None, None, PAGE_SIZE, H_DA, dh2),
                            lambda b, p, pt: (layer, pt[b, p * group + g], 0, 0, 0))

    pages = [page_spec(g) for g in range(group)]
    grid_spec = pltpu.PrefetchScalarGridSpec(
        num_scalar_prefetch=1,
        grid=(bd, n_steps),
        in_specs=[row_spec, row_spec, row_spec, row_spec,
                  pl.BlockSpec((1, dh2), lambda b, p, pt: (0, 0)),
                  tok_spec, tok_spec, tok_spec] + pages + pages,
        out_specs=tok_spec,
        scratch_shapes=[pltpu.VMEM((2 * H_DA, 1), F32), pltpu.VMEM((2 * H_DA, 1), F32),
                        pltpu.VMEM((2 * H_DA, dh2), F32)],
    )
    return pl.pallas_call(
        functools.partial(_attn_sample_kernel, lam_init=lam_init, n_steps=n_steps, group=group),
        grid_spec=grid_spec,
        out_shape=jax.ShapeDtypeStruct((bd, H_DA, dh2), F32),
        compiler_params=_params(("parallel", "arbitrary")),
        name="attn_sample",
    )(page_table, *lam_rows, subln_g.reshape(1, dh2), q, k_new, v_new,
      *([cache_k] * group), *([cache_v] * group))


def _mlstm_out(h, g, om):
    mu = jnp.mean(h, axis=-1, keepdims=True)
    hc = h - mu
    var = jnp.mean(hc * hc, axis=-1, keepdims=True)
    return hc * lax.rsqrt(var + NORM_EPS) * g * _sigmoid(om)


def _mlstm_prompt_kernel(big_ref, bfg_ref, q_ref, k_ref, v_ref, om_ref, gc_ref, gr_ref, ng_ref,
                         hm_ref, c_out, n_out, m_out, c_sc, n_sc, m_sc, *, n_chunks):
    ci = pl.program_id(1)
    L = MLSTM_CHUNK

    @pl.when(ci == 0)
    def _():
        c_sc[...] = jnp.zeros(c_sc.shape, F32)
        n_sc[...] = jnp.zeros(n_sc.shape, F32)
        m_sc[...] = jnp.zeros(m_sc.shape, F32)

    ri = lax.broadcasted_iota(jnp.int32, (L, L), 0)
    cj = lax.broadcasted_iota(jnp.int32, (L, L), 1)
    lower = ri >= cj
    gc = gc_ref[...]
    gr = gr_ref[...]
    for h in range(H_M):
        hs = slice(h * DH_M, (h + 1) * DH_M)
        ig_c = gc[:, h:h + 1] + big_ref[h]
        lf_c = _log_sigmoid(gc[:, H_M + h:H_M + h + 1] + bfg_ref[h])
        ig_r = gr[h:h + 1, :] + big_ref[h]
        lf_r = _log_sigmoid(gr[H_M + h:H_M + h + 1, :] + bfg_ref[h])
        b_col = jnp.sum(jnp.where(lower, lf_r, 0.0), axis=1, keepdims=True)
        b_row = jnp.sum(jnp.where(cj >= ri, lf_c, 0.0), axis=0, keepdims=True)
        dm = jnp.where(lower, b_col - b_row + ig_r, -jnp.inf)
        m_prev = m_sc[h:h + 1, 0:1]
        inter = b_col + m_prev
        m_t = jnp.maximum(inter, jnp.max(dm, axis=1, keepdims=True))
        w_intra = jnp.exp(dm - m_t)
        w_inter = jnp.exp(inter - m_t)
        q = q_ref[:, hs]
        ks = k_ref[:, hs] * (DH_M ** -0.5)
        v = v_ref[:, hs]
        c_prev = c_sc[h]
        n_prev = n_sc[h:h + 1, :]
        sc = w_intra * lax.dot_general(q, ks, (((1,), (1,)), ((), ())), preferred_element_type=F32)
        num = (jnp.dot(sc.astype(BF16), v, preferred_element_type=F32)
               + w_inter * lax.dot_general(q, c_prev.astype(BF16), (((1,), (1,)), ((), ())),
                                           preferred_element_type=F32))
        den = (jnp.sum(sc, axis=1, keepdims=True)
               + w_inter * jnp.sum(q.astype(F32) * n_prev, axis=1, keepdims=True))
        hh = num / jnp.maximum(jnp.abs(den), jnp.exp(-m_t))
        m_new = m_t[L - 1:L, :]
        decay = w_inter[L - 1:L, :]
        wk = jnp.exp(b_col[L - 1:L, :] - b_col + ig_c - m_new)
        wv = (wk * v.astype(F32)).astype(BF16)
        c_sc[h] = decay * c_prev + lax.dot_general(wv, ks, (((0,), (0,)), ((), ())),
                                                   preferred_element_type=F32)
        n_sc[h:h + 1, :] = decay * n_prev + jnp.sum(wk * ks.astype(F32), axis=0, keepdims=True)
        m_sc[h:h + 1, :] = jnp.broadcast_to(m_new, (1, LANES))
        hm_ref[:, hs] = _mlstm_out(hh, ng_ref[:, hs], om_ref[:, hs].astype(F32)).astype(hm_ref.dtype)

    @pl.when(ci == n_chunks - 1)
    def _():
        c_out[...] = c_sc[...]
        n_out[...] = n_sc[0:H_M, :]
        m_out[...] = m_sc[0:H_M, :]


def _mlstm_prompt(um, gates, gates_t, b_ig, b_fg, norm_g, batch, seq):
    L = MLSTM_CHUNK
    nc = seq // L
    smem = pl.BlockSpec(memory_space=pltpu.SMEM)

    def col(cb):
        return pl.BlockSpec((L, D_MLSTM), lambda b, c: (b * nc + c, cb))

    return pl.pallas_call(
        functools.partial(_mlstm_prompt_kernel, n_chunks=nc),
        grid=(batch, nc),
        in_specs=[smem, smem, col(0), col(1), col(2), col(3),
                  pl.BlockSpec((L, LANES), lambda b, c: (b * nc + c, 0)),
                  pl.BlockSpec((None, SUBLANES, L), lambda b, c: (b, 0, c)),
                  pl.BlockSpec((1, D_MLSTM), lambda b, c: (0, 0))],
        out_specs=[pl.BlockSpec((L, D_MLSTM), lambda b, c: (b * nc + c, 0)),
                   pl.BlockSpec((None, H_M, DH_M, DH_M), lambda b, c: (b, 0, 0, 0)),
                   pl.BlockSpec((None, H_M, DH_M), lambda b, c: (b, 0, 0)),
                   pl.BlockSpec((None, H_M, LANES), lambda b, c: (b, 0, 0))],
        out_shape=[jax.ShapeDtypeStruct((batch * seq, D_MLSTM), BF16),
                   jax.ShapeDtypeStruct((batch, H_M, DH_M, DH_M), F32),
                   jax.ShapeDtypeStruct((batch, H_M, DH_M), F32),
                   jax.ShapeDtypeStruct((batch, H_M, LANES), F32)],
        scratch_shapes=[pltpu.VMEM((H_M, DH_M, DH_M), F32), pltpu.VMEM((SUBLANES, DH_M), F32),
                        pltpu.VMEM((SUBLANES, LANES), F32)],
        compiler_params=_params(("parallel", "arbitrary")),
        name="mlstm_prompt",
    )(b_ig, b_fg, um, um, um, um, gates, gates_t, norm_g.reshape(1, D_MLSTM))


def _mlstm_sample_kernel(big_ref, bfg_ref, u_ref, vcol_ref, g_ref, ng_ref, c_ref, n_ref, m_ref,
                         hm_ref, c_out, n_out, m_out):
    u = u_ref[...]
    g = g_ref[...]
    for h in range(H_M):
        def part(i):
            return u[:, i * D_MLSTM + h * DH_M:i * D_MLSTM + (h + 1) * DH_M]
        q, k, v, om = part(0), part(1), part(2), part(3)
        ks = k * (DH_M ** -0.5)
        ig = g[:, h:h + 1] + big_ref[h]
        lf = _log_sigmoid(g[:, H_M + h:H_M + h + 1] + bfg_ref[h])
        m_prev = m_ref[:, h:h + 1]
        inter = lf + m_prev
        m_t = jnp.maximum(inter, ig)
        w_intra = jnp.exp(ig - m_t)
        w_inter = jnp.exp(inter - m_t)
        c_prev = c_ref[h]
        n_prev = n_ref[h:h + 1, :]
        q8 = jnp.broadcast_to(q, (2 * SUBLANES, DH_M)).astype(BF16)
        cq = lax.dot_general(q8, c_prev.astype(BF16), (((1,), (1,)), ((), ())),
                             preferred_element_type=F32)[0:1, :]
        sc = w_intra * jnp.sum(q * ks, axis=-1, keepdims=True)
        num = sc * v + w_inter * cq
        den = sc + w_inter * jnp.sum(n_prev * q, axis=-1, keepdims=True)
        hh = num / jnp.maximum(jnp.abs(den), jnp.exp(-m_t))
        c_out[h] = w_inter * c_prev + (w_intra * vcol_ref[h]) * ks
        n_out[h:h + 1, :] = w_inter * n_prev + w_intra * ks
        m_out[h:h + 1, :] = jnp.broadcast_to(m_t, (1, LANES))
        hs = slice(h * DH_M, (h + 1) * DH_M)
        hm_ref[:, hs] = _mlstm_out(hh, ng_ref[:, hs], om)


def _mlstm_sample(um, gates, b_ig, b_fg, norm_g, c_state, n_state, m_state):
    bd = um.shape[0]
    smem = pl.BlockSpec(memory_space=pltpu.SMEM)
    u4 = um.reshape(bd, 1, 4 * D_MLSTM)
    vcol = um[:, 2 * D_MLSTM:3 * D_MLSTM].reshape(bd, H_M, DH_M, 1)
    return pl.pallas_call(
        _mlstm_sample_kernel,
        grid=(bd,),
        in_specs=[smem, smem,
                  pl.BlockSpec((None, 1, 4 * D_MLSTM), lambda b: (b, 0, 0)),
                  pl.BlockSpec((None, H_M, DH_M, 1), lambda b: (b, 0, 0, 0)),
                  pl.BlockSpec((None, 1, LANES), lambda b: (b, 0, 0)),
                  pl.BlockSpec((1, D_MLSTM), lambda b: (0, 0)),
                  pl.BlockSpec((None, H_M, DH_M, DH_M), lambda b: (b, 0, 0, 0)),
                  pl.BlockSpec((None, H_M, DH_M), lambda b: (b, 0, 0)),
                  pl.BlockSpec((None, 1, H_M), lambda b: (b, 0, 0))],
        out_specs=[pl.BlockSpec((None, 1, D_MLSTM), lambda b: (b, 0, 0)),
                   pl.BlockSpec((None, H_M, DH_M, DH_M), lambda b: (b, 0, 0, 0)),
                   pl.BlockSpec((None, H_M, DH_M), lambda b: (b, 0, 0)),
                   pl.BlockSpec((None, H_M, LANES), lambda b: (b, 0, 0))],
        out_shape=[jax.ShapeDtypeStruct((bd, 1, D_MLSTM), F32),
                   jax.ShapeDtypeStruct((bd, H_M, DH_M, DH_M), F32),
                   jax.ShapeDtypeStruct((bd, H_M, DH_M), F32),
                   jax.ShapeDtypeStruct((bd, H_M, LANES), F32)],
        compiler_params=_params(("parallel",)),
        name="mlstm_sample",
    )(b_ig, b_fg, u4, vcol, gates.reshape(bd, 1, LANES), norm_g.reshape(1, D_MLSTM),
      c_state, n_state, m_state.reshape(bd, 1, H_M))


def _rglru_coeffs(xc, wra_ref, bra_ref, wrx_ref, brx_ref, lam_ref):
    rs, is_ = [], []
    for n in range(RG_BLOCKS):
        xb = xc[:, n * RG_BW:(n + 1) * RG_BW].astype(BF16)
        rs.append(jnp.dot(xb, wra_ref[n], preferred_element_type=F32))
        is_.append(jnp.dot(xb, wrx_ref[n], preferred_element_type=F32))
    r = _sigmoid(jnp.concatenate(rs, axis=-1) + bra_ref[...])
    i = _sigmoid(jnp.concatenate(is_, axis=-1) + brx_ref[...])
    log_a = -RG_C * r * _softplus(-lam_ref[...])
    a = jnp.exp(log_a)
    mult = jnp.sqrt(1.0 - a * a)
    return a, mult * i * xc


def _rglru_prompt_kernel(x_ref, gate_ref, cw_ref, cb_ref, wra_ref, bra_ref, wrx_ref, brx_ref, lam_ref,
                         y_ref, conv_out, h_out, xbuf, a_sc, b_sc, h_sc, *, tt, n_tiles):
    ti = pl.program_id(1)
    pad = SUBLANES

    @pl.when(ti == 0)
    def _():
        xbuf[0:pad, :] = jnp.zeros((pad, D_RG), F32)
        h_sc[...] = jnp.zeros(h_sc.shape, F32)

    xbuf[pad:pad + tt, :] = x_ref[...].astype(F32)
    xc = cb_ref[...] + sum(xbuf[pad - (CONV_W - 1) + j:pad - (CONV_W - 1) + j + tt, :] * cw_ref[j:j + 1, :]
                           for j in range(CONV_W))
    a, b = _rglru_coeffs(xc, wra_ref, bra_ref, wrx_ref, brx_ref, lam_ref)
    a_sc[...] = a
    b_sc[...] = b

    def step(t, h):
        h = a_sc[pl.ds(t, 1), :] * h + b_sc[pl.ds(t, 1), :]
        b_sc[pl.ds(t, 1), :] = h
        return h

    h_last = lax.fori_loop(0, tt, step, h_sc[...], unroll=8)
    h_sc[...] = h_last
    y_ref[...] = (b_sc[...] * _gelu_tanh(gate_ref[...].astype(F32))).astype(y_ref.dtype)
    tail = xbuf[tt:tt + pad, :]
    xbuf[0:pad, :] = tail

    @pl.when(ti == n_tiles - 1)
    def _():
        conv_out[...] = tail[pad - (CONV_W - 1):, :]
        h_out[...] = h_last


def _rglru_weights(conv_w, conv_b, w_ra, b_ra, w_rx, b_rx, lam):
    row = lambda z: z.reshape(1, D_RG).astype(F32)
    return (conv_w.astype(F32), row(conv_b), w_ra.astype(BF16), row(b_ra), w_rx.astype(BF16), row(b_rx), row(lam))


def _rglru_weight_specs():
    zeros2 = (lambda *a: (0, 0))
    zeros3 = (lambda *a: (0, 0, 0))
    return [pl.BlockSpec((CONV_W, D_RG), zeros2), pl.BlockSpec((1, D_RG), zeros2),
            pl.BlockSpec((RG_BLOCKS, RG_BW, RG_BW), zeros3), pl.BlockSpec((1, D_RG), zeros2),
            pl.BlockSpec((RG_BLOCKS, RG_BW, RG_BW), zeros3), pl.BlockSpec((1, D_RG), zeros2),
            pl.BlockSpec((1, D_RG), zeros2)]


def _rglru_prompt(urg, weights, batch, seq, tt=256):
    nt = seq // tt
    return pl.pallas_call(
        functools.partial(_rglru_prompt_kernel, tt=tt, n_tiles=nt),
        grid=(batch, nt),
        in_specs=[pl.BlockSpec((tt, D_RG), lambda b, t: (b * nt + t, 0)),
                  pl.BlockSpec((tt, D_RG), lambda b, t: (b * nt + t, 1))] + _rglru_weight_specs(),
        out_specs=[pl.BlockSpec((tt, D_RG), lambda b, t: (b * nt + t, 0)),
                   pl.BlockSpec((None, CONV_W - 1, D_RG), lambda b, t: (b, 0, 0)),
                   pl.BlockSpec((None, 1, D_RG), lambda b, t: (b, 0, 0))],
        out_shape=[jax.ShapeDtypeStruct((batch * seq, D_RG), BF16),
                   jax.ShapeDtypeStruct((batch, CONV_W - 1, D_RG), F32),
                   jax.ShapeDtypeStruct((batch, 1, D_RG), F32)],
        scratch_shapes=[pltpu.VMEM((tt + SUBLANES, D_RG), F32), pltpu.VMEM((tt, D_RG), F32),
                        pltpu.VMEM((tt, D_RG), F32), pltpu.VMEM((1, D_RG), F32)],
        compiler_params=_params(("parallel", "arbitrary")),
        name="rglru_prompt",
    )(urg, urg, *weights)


def _rglru_sample_kernel(x_ref, gate_ref, p0_ref, p1_ref, p2_ref, h0_ref, cw_ref, cb_ref, wra_ref, bra_ref,
                         wrx_ref, brx_ref, lam_ref, y_ref, h_out):
    x = x_ref[...]
    xc = cb_ref[...] + (p0_ref[...] * cw_ref[0:1, :] + p1_ref[...] * cw_ref[1:2, :]
                        + p2_ref[...] * cw_ref[2:3, :] + x * cw_ref[3:4, :])
    a, b = _rglru_coeffs(xc, wra_ref, bra_ref, wrx_ref, brx_ref, lam_ref)
    h = a * h0_ref[...] + b
    h_out[...] = h
    y_ref[...] = h * _gelu_tanh(gate_ref[...])


def _rglru_sample(x, gate, conv_prev, h0, weights):
    bd = x.shape[0]
    full = pl.BlockSpec((bd, D_RG), lambda i: (0, 0))
    return pl.pallas_call(
        _rglru_sample_kernel,
        grid=(1,),
        in_specs=[full] * 6 + _rglru_weight_specs(),
        out_specs=[full, full],
        out_shape=[jax.ShapeDtypeStruct((bd, D_RG), F32)] * 2,
        compiler_params=_params(("arbitrary",)),
        name="rglru_sample",
    )(x, gate, conv_prev[:, 0], conv_prev[:, 1], conv_prev[:, 2], h0, *weights)


def _rope_tables(pos):
    inv = 1.0 / (ROPE_THETA ** (jnp.arange(0, DH_DA, 2, dtype=F32) / DH_DA))
    ang = pos.astype(F32)[:, None] * inv[None, :]
    ang = jnp.concatenate([ang, ang], axis=-1)
    sign = jnp.concatenate([-jnp.ones((DH_DA // 2,), F32), jnp.ones((DH_DA // 2,), F32)])
    return jnp.cos(ang), jnp.sin(ang) * sign[None, :]


def kernel(x_prompt, x_sample, cache_k, cache_v, page_table, state_mlstm_c, state_mlstm_n, state_mlstm_m, state_conv, state_rglru_h, norm_mix_g, w_in, w_out, lam_q1, lam_k1, lam_q2, lam_k2, attn_subln_g, b_ig, b_fg, mlstm_norm_g, conv_w, conv_b, w_ra, b_ra, w_rx, b_rx, rg_lambda, norm_mlp_g, w_up, w_down, final_norm_g):
    bp, sp, _ = x_prompt.shape
    bd, td, _ = x_sample.shape
    assert td == 1
    past = page_table.shape[1] * PAGE_SIZE
    mp = bp * sp
    dh2 = 2 * DH_DA
    cos_p, sin_p = _rope_tables(jnp.arange(sp))
    cos_s, sin_s = _rope_tables(jnp.full((bd,), past))

    TM = 1024
    TNC = 512

    xp = x_prompt.reshape(mp, D_MODEL)
    xs = x_sample.reshape(bd, D_MODEL)
    k_all = v_all = None
    outs = {n: [] for n in ("ks", "vs", "cp", "np", "mp", "cs", "ns", "ms", "cvp", "cvs", "hp", "hs")}
    for l in range(DEPTH):
        lam_init = 0.8 - 0.6 * math.exp(-0.3 * l)
        lam_rows = [z[l].reshape(1, DH_DA).astype(F32) for z in (lam_q1, lam_k1, lam_q2, lam_k2)]
        rg_w = _rglru_weights(conv_w[l], conv_b[l], w_ra[l], b_ra[l], w_rx[l], b_rx[l], rg_lambda[l])
        w_tail = w_in[l][:, D_MAIN:]
        wg = jnp.pad(w_tail[:, :2 * H_M], ((0, 0), (0, LANES - 2 * H_M))).astype(BF16)
        wr = w_tail[:, 2 * H_M:].astype(BF16)

        hs_n = _rmsnorm(xs, norm_mix_g[l], NORM_EPS, F32, tm=bd)
        u_s, w_in_b = _cast_matmul(hs_n, w_in, l, D_MAIN, tn=TNC, tk=D_MODEL, name="s_proj_in")
        mms = functools.partial(_matmul, hs_n, tm=bd, tk=D_MODEL, out_dtypes=(F32,))
        gates_s = mms(wg, tn=LANES, name="s_proj_g")
        urg_s = mms(wr, tn=1024, name="s_proj_r")
        qk_s = _rope_rows(u_s[:, :2 * D_ATTN], cos_s, sin_s)
        q_s, k_s = qk_s[:, :D_ATTN], qk_s[:, D_ATTN:]
        v_s = u_s[:, 2 * D_ATTN:3 * D_ATTN]
        um_s = u_s[:, 3 * D_ATTN:]

        hn = _rmsnorm(xp, norm_mix_g[l], NORM_EPS, BF16, tm=512)
        mm = functools.partial(_matmul, hn, tm=TM, tk=D_MODEL)
        prev_k = None if k_all is None else [k_all, None]
        prev_v = None if v_all is None else [v_all, None]
        q = mm(w_in_b, n=D_ATTN, w_col0=0, tn=512, out_dtypes=(BF16,), rope=(cos_p, sin_p),
               rope_scale=ATTN_QSCALE, name="proj_q")
        k_all, k16 = mm(w_in_b, n=D_ATTN, w_col0=D_ATTN, tn=512, out_dtypes=(F32, BF16), rope=(cos_p, sin_p),
                        stacked=[True, False], prev=prev_k, name="proj_k")
        v_all, v16 = mm(w_in_b, n=D_ATTN, w_col0=2 * D_ATTN, tn=512, out_dtypes=(F32, BF16),
                        stacked=[True, False], prev=prev_v, name="proj_v")
        um = mm(w_in_b, n=4 * D_MLSTM, w_col0=3 * D_ATTN, tn=1024, out_dtypes=(BF16,), name="proj_m")
        urg = mm(wr, tn=1024, out_dtypes=(BF16,), name="proj_r")
        gates = mm(wg, tn=LANES, out_dtypes=(F32,), name="proj_g")
        gates_t = gates[:, :SUBLANES].reshape(bp, sp, SUBLANES).transpose(0, 2, 1)
        att = _attn_prompt(q, k16, v16, lam_rows, attn_subln_g[l], lam_init, bp, sp)
        hm, c_p, n_p, m_p = _mlstm_prompt(um, gates, gates_t, b_ig[l], b_fg[l], mlstm_norm_g[l], bp, sp)
        hr, cv_p, h_p = _rglru_prompt(urg, rg_w, bp, sp)

        att_s = _attn_sample(page_table, q_s.reshape(bd, H_DA, dh2), k_s.reshape(bd, H_DA, dh2),
                             v_s.reshape(bd, H_DA, dh2), cache_k, cache_v, l, lam_rows,
                             attn_subln_g[l], lam_init)
        hm_s, c_s, n_s, m_s = _mlstm_sample(um_s, gates_s, b_ig[l], b_fg[l], mlstm_norm_g[l],
                                            state_mlstm_c[l], state_mlstm_n[l], state_mlstm_m[l])
        x_rg = urg_s[:, :D_RG]
        hr_s, h_s = _rglru_sample(x_rg, urg_s[:, D_RG:], state_conv[l], state_rglru_h[l], rg_w)
        cat_s = jnp.concatenate([att_s.reshape(bd, D_ATTN), hm_s.reshape(bd, D_MLSTM), hr_s], axis=1)
        x1s, wo = _cast_matmul(cat_s, w_out, l, D_MODEL, tn=TNC, tk=D_MODEL, residual=xs, name="s_proj_out")
        hs_n2 = _rmsnorm(x1s, norm_mlp_g[l], NORM_EPS, F32, tm=bd)
        act_s, wu = _cast_matmul(hs_n2, w_up, l, D_FF, tn=TNC, tk=D_MODEL, act="relu2", name="s_mlp_up")
        xs, wd = _cast_matmul(act_s, w_down, l, D_MODEL, tn=TNC, tk=D_MODEL, residual=x1s, name="s_mlp_down")

        x1 = _matmul([att, hm, hr], wo, tm=TM, tn=512, tk=D_MODEL, out_dtypes=(F32,), residual=xp,
                     name="proj_out")
        hn2 = _rmsnorm(x1, norm_mlp_g[l], NORM_EPS, BF16, tm=512)
        act = _matmul(hn2, wu, tm=TM, tn=1024, tk=D_MODEL, out_dtypes=(BF16,), act="relu2", name="mlp_up")
        xp = _matmul(act, wd, tm=TM, tn=1024, tk=2048, out_dtypes=(F32,), residual=x1, name="mlp_down")

        outs["ks"].append(k_s.reshape(bd, td, H_DA, dh2))
        outs["vs"].append(v_s.reshape(bd, td, H_DA, dh2))
        outs["cp"].append(c_p)
        outs["np"].append(n_p)
        outs["mp"].append(m_p[:, :, 0])
        outs["cs"].append(c_s)
        outs["ns"].append(n_s)
        outs["ms"].append(m_s[:, :, 0])
        outs["cvp"].append(cv_p)
        outs["cvs"].append(jnp.concatenate([state_conv[l][:, 1:], x_rg[:, None, :]], axis=1))
        outs["hp"].append(h_p.reshape(bp, D_RG))
        outs["hs"].append(h_s)

    y_prompt = _rmsnorm(xp, final_norm_g, NORM_EPS, F32, tm=512).reshape(bp, sp, D_MODEL)
    y_sample = _rmsnorm(xs, final_norm_g, NORM_EPS, F32, tm=bd).reshape(bd, td, D_MODEL)
    st = lambda n: jnp.stack(outs[n])
    return (y_prompt, y_sample, k_all.reshape(DEPTH, bp, sp, H_DA, dh2), v_all.reshape(DEPTH, bp, sp, H_DA, dh2),
            st("ks"), st("vs"), st("cp"), st("np"), st("mp"),
            st("cs"), st("ns"), st("ms"), st("cvp"), st("cvs"), st("hp"), st("hs"))
```

```python
import functools
import math

import jax
import jax.numpy as jnp
from jax import lax
from jax.experimental import pallas as pl
from jax.experimental.pallas import tpu as pltpu

D_MODEL = 4096
DEPTH = 2
PAGE_SIZE = 128
D_ATTN = D_MODEL // 2
D_MLSTM = D_MODEL // 4
D_RG = D_MODEL - D_ATTN - D_MLSTM
DH_DA = 128
H_DA = D_ATTN // (2 * DH_DA)
H_M = 4
DH_M = D_MLSTM // H_M
RG_BLOCKS = 8
RG_BW = D_RG // RG_BLOCKS
CONV_W = 4
RG_C = 8.0
D_FF = 4 * D_MODEL
ROPE_THETA = 10000.0
MLSTM_CHUNK = 128
NORM_EPS = 1e-6
SUBLN_EPS = 1e-5
NEG = -1e30
D_MAIN = 3 * D_ATTN + 4 * D_MLSTM
ATTN_QSCALE = DH_DA ** -0.5 * math.log2(math.e)

LANES = 128
SUBLANES = 8
VMEM_LIMIT = 56 * 1024 * 1024

F32 = jnp.float32
BF16 = jnp.bfloat16


def _params(sem):
    return pltpu.CompilerParams(dimension_semantics=sem, vmem_limit_bytes=VMEM_LIMIT)


def _sigmoid(x):
    return 1.0 / (1.0 + jnp.exp(-x))


def _softplus(x):
    return jnp.maximum(x, 0.0) + jnp.log1p(jnp.exp(-jnp.abs(x)))


def _log_sigmoid(x):
    return -_softplus(-x)


def _gelu_tanh(x):
    c = math.sqrt(2.0 / math.pi)
    return 0.5 * x * (1.0 + jnp.tanh(c * (x + 0.044715 * (x * x * x))))


def _lambda(lq1, lk1, lq2, lk2, lam_init):
    a = jnp.exp(jnp.sum(lq1[...] * lk1[...], axis=-1, keepdims=True))
    b = jnp.exp(jnp.sum(lq2[...] * lk2[...], axis=-1, keepdims=True))
    return a - b + lam_init


def _rotate(x, cos, sin_signed):
    return x * cos + pltpu.roll(x, DH_DA // 2, axis=1) * sin_signed


def _rmsnorm_kernel(x_ref, g_ref, o_ref, *, eps):
    x = x_ref[...].astype(F32)
    ms = jnp.mean(x * x, axis=-1, keepdims=True)
    o_ref[...] = (x * lax.rsqrt(ms + eps) * g_ref[...]).astype(o_ref.dtype)


def _rmsnorm(x, g, eps, out_dtype, tm):
    m, d = x.shape
    return pl.pallas_call(
        functools.partial(_rmsnorm_kernel, eps=eps),
        grid=(m // tm,),
        in_specs=[pl.BlockSpec((tm, d), lambda i: (i, 0)),
                  pl.BlockSpec((1, d), lambda i: (0, 0))],
        out_specs=pl.BlockSpec((tm, d), lambda i: (i, 0)),
        out_shape=jax.ShapeDtypeStruct((m, d), out_dtype),
        compiler_params=_params(("parallel",)),
        name="rmsnorm",
    )(x, g.reshape(1, d).astype(F32))


def _mm_kernel(*refs, nk, seg, w_t, out_depths, n_prev, has_res, has_rope, rope_scale, act, tn):
    it = iter(refs)
    x_refs = [next(it) for _ in seg]
    w_ref = next(it)
    cos_ref = next(it) if has_rope else None
    sin_ref = next(it) if has_rope else None
    res_ref = next(it) if has_res else None
    prev_refs = [next(it) for _ in range(n_prev)]
    o_refs = [next(it) for _ in out_depths]
    acc_ref = next(it) if nk > 1 else None

    if w_t:
        part = lax.dot_general(x_refs[0][...], w_ref[...], (((1,), (1,)), ((), ())),
                               preferred_element_type=F32)
    elif len(seg) == 1:
        part = jnp.dot(x_refs[0][...].astype(BF16), w_ref[...], preferred_element_type=F32)
    else:
        part, off = None, 0
        for x_ref, width in zip(x_refs, seg):
            d = jnp.dot(x_ref[...], w_ref[off:off + width, :], preferred_element_type=F32)
            part = d if part is None else part + d
            off += width

    def store(sl, val):
        for o, depth in zip(o_refs, out_depths):
            if depth is None:
                o[:, sl] = val.astype(o.dtype)
            else:
                o[depth - 1, :, sl] = val.astype(o.dtype)

    def epilogue(acc):
        pi = 0
        for o, depth in zip(o_refs, out_depths):
            if depth is not None and depth > 1:
                o[0:depth - 1] = prev_refs[pi][...]
                pi += 1
        if has_rope:
            cos = cos_ref[...]
            sin = sin_ref[...]
            for g in range(tn // DH_DA):
                sl = slice(g * DH_DA, (g + 1) * DH_DA)
                store(sl, _rotate(acc[:, sl], cos, sin) * rope_scale)
            return
        if act == "relu2":
            r = jnp.maximum(acc, 0.0)
            acc = r * r
        if has_res:
            acc = res_ref[...] + acc
        store(slice(None), acc)

    if nk == 1:
        epilogue(part)
    else:
        k = pl.program_id(2)

        @pl.when(k == 0)
        def _():
            acc_ref[...] = part

        @pl.when(k > 0)
        def _():
            acc_ref[...] += part

        @pl.when(k == nk - 1)
        def _():
            epilogue(acc_ref[...])


def _matmul(xs, w, *, tm, tn, tk, out_dtypes, n=None, w_col0=0, w_t=False, residual=None, rope=None,
            rope_scale=1.0, act=None, stacked=None, prev=None, name="matmul"):
    xs = list(xs) if isinstance(xs, (list, tuple)) else [xs]
    seg = [x.shape[1] for x in xs]
    m = xs[0].shape[0]
    kdim = sum(seg)
    n = w.shape[0 if w_t else 1] - w_col0 if n is None else n
    nk = kdim // tk
    assert len(xs) == 1 or (nk == 1 and not w_t)
    jb = w_col0 // tn
    stacked = stacked or [False] * len(out_dtypes)
    prev = prev or [None] * len(out_dtypes)

    if len(xs) == 1:
        in_specs = [pl.BlockSpec((tm, tk), lambda i, j, k: (i, k))]
    else:
        in_specs = [pl.BlockSpec((tm, s), lambda i, j, k: (i, 0)) for s in seg]
    if w_t:
        in_specs.append(pl.BlockSpec((tn, tk), lambda i, j, k: (j + jb, k)))
    else:
        in_specs.append(pl.BlockSpec((tk, tn), lambda i, j, k: (k, j + jb)))
    args = xs + [w]
    if rope is not None:
        nr = rope[0].shape[0] // tm
        spec = pl.BlockSpec((tm, DH_DA), lambda i, j, k: (i % nr, 0))
        in_specs += [spec, spec]
        args += [rope[0], rope[1]]
    if residual is not None:
        in_specs.append(pl.BlockSpec((tm, tn), lambda i, j, k: (i, j)))
        args.append(residual)
    out_depths, out_specs, out_shapes, n_prev = [], [], [], 0
    for dt, st, pv in zip(out_dtypes, stacked, prev):
        if not st:
            out_depths.append(None)
            out_specs.append(pl.BlockSpec((tm, tn), lambda i, j, k: (i, j)))
            out_shapes.append(jax.ShapeDtypeStruct((m, n), dt))
            continue
        depth = 1 if pv is None else pv.shape[0] + 1
        out_depths.append(depth)
        out_specs.append(pl.BlockSpec((depth, tm, tn), lambda i, j, k: (0, i, j)))
        out_shapes.append(jax.ShapeDtypeStruct((depth, m, n), dt))
        if pv is not None:
            in_specs.append(pl.BlockSpec((depth - 1, tm, tn), lambda i, j, k: (0, i, j)))
            args.append(pv)
            n_prev += 1
    outs = pl.pallas_call(
        functools.partial(_mm_kernel, nk=nk, seg=seg, w_t=w_t, out_depths=out_depths, n_prev=n_prev,
                          has_res=residual is not None, has_rope=rope is not None, rope_scale=rope_scale,
                          act=act, tn=tn),
        grid=(m // tm, n // tn, nk),
        in_specs=in_specs,
        out_specs=out_specs,
        out_shape=out_shapes,
        scratch_shapes=[pltpu.VMEM((tm, tn), F32)] if nk > 1 else [],
        compiler_params=_params(("parallel", "parallel", "arbitrary")),
        name=name,
    )(*args)
    return outs[0] if len(out_dtypes) == 1 else outs


def _cast_mm_kernel(*refs, nk, has_res, act):
    it = iter(refs)
    x_ref = next(it)
    w_ref = next(it)
    res_ref = next(it) if has_res else None
    wb_ref = next(it)
    y_ref = next(it)
    k = pl.program_id(1)
    wb = w_ref[...].astype(BF16)
    wb_ref[...] = wb
    part = jnp.dot(x_ref[...].astype(BF16), wb, preferred_element_type=F32)

    @pl.when(k == 0)
    def _():
        y_ref[...] = part

    @pl.when(k > 0)
    def _():
        y_ref[...] += part

    if act is not None or has_res:
        @pl.when(k == nk - 1)
        def _():
            acc = y_ref[...]
            if act == "relu2":
                r = jnp.maximum(acc, 0.0)
                acc = r * r
            if has_res:
                acc = res_ref[...] + acc
            y_ref[...] = acc


def _cast_matmul(x, w_all, layer, n, *, tn, tk, residual=None, act=None, name="cast_matmul"):
    m, kdim = x.shape
    nk = kdim // tk
    in_specs = [pl.BlockSpec((m, tk), lambda j, k: (0, k)),
                pl.BlockSpec((None, tk, tn), lambda j, k: (layer, k, j))]
    args = [x, w_all]
    if residual is not None:
        in_specs.append(pl.BlockSpec((m, tn), lambda j, k: (0, j)))
        args.append(residual)
    wb, y = pl.pallas_call(
        functools.partial(_cast_mm_kernel, nk=nk, has_res=residual is not None, act=act),
        grid=(n // tn, nk),
        in_specs=in_specs,
        out_specs=[pl.BlockSpec((tk, tn), lambda j, k: (k, j)),
                   pl.BlockSpec((m, tn), lambda j, k: (0, j))],
        out_shape=[jax.ShapeDtypeStruct((kdim, n), BF16), jax.ShapeDtypeStruct((m, n), F32)],
        compiler_params=_params(("parallel", "arbitrary")),
        name=name,
    )(*args)
    return y, wb


def _nt_dot(x, wt):
    return lax.dot_general(x, wt, (((1,), (1,)), ((), ())), preferred_element_type=F32)


def _cast_mm_t_kernel(x_ref, w_ref, wb_ref, y_ref):
    wb = w_ref[...].astype(BF16)
    wb_ref[...] = wb
    y_ref[...] = _nt_dot(x_ref[...].astype(BF16), wb)


def _cast_matmul_t(x, wt_all, layer, n, *, tn, name):
    m, kdim = x.shape
    wb, y = pl.pallas_call(
        _cast_mm_t_kernel,
        grid=(n // tn,),
        in_specs=[pl.BlockSpec((m, kdim), lambda j: (0, 0)),
                  pl.BlockSpec((None, tn, kdim), lambda j: (layer, j, 0))],
        out_specs=[pl.BlockSpec((tn, kdim), lambda j: (j, 0)),
                   pl.BlockSpec((m, tn), lambda j: (0, j))],
        out_shape=[jax.ShapeDtypeStruct((n, kdim), BF16), jax.ShapeDtypeStruct((m, n), F32)],
        compiler_params=_params(("parallel",)),
        name=name,
    )(x, wt_all)
    return y, wb


def _cast_tail_kernel(x_ref, a_ref, b_ref, wr_ref, wg_ref, yr_ref, yg_ref):
    j = pl.program_id(0)
    xb = x_ref[...].astype(BF16)
    a = a_ref[...]
    tile = jnp.concatenate([a[SUBLANES:], b_ref[...]], axis=0).astype(BF16)
    wr_ref[...] = tile
    yr_ref[...] = _nt_dot(xb, tile)

    @pl.when(j == 0)
    def _():
        pad = jnp.zeros((LANES - SUBLANES, a.shape[1]), F32)
        g = jnp.concatenate([a[:SUBLANES], pad], axis=0).astype(BF16)
        wg_ref[...] = g
        yg_ref[...] = _nt_dot(xb, g)


def _cast_tail(x, wt_all, layer, *, tn, name):
    assert 2 * H_M == SUBLANES
    m, kdim = x.shape
    n = 2 * D_RG
    blk0 = D_MAIN // tn
    return pl.pallas_call(
        _cast_tail_kernel,
        grid=(n // tn,),
        in_specs=[pl.BlockSpec((m, kdim), lambda j: (0, 0)),
                  pl.BlockSpec((None, tn, kdim), lambda j: (layer, blk0 + j, 0)),
                  pl.BlockSpec((None, SUBLANES, kdim), lambda j: (layer, (blk0 + j + 1) * (tn // SUBLANES), 0))],
        out_specs=[pl.BlockSpec((tn, kdim), lambda j: (j, 0)),
                   pl.BlockSpec((LANES, kdim), lambda j: (0, 0)),
                   pl.BlockSpec((m, tn), lambda j: (0, j)),
                   pl.BlockSpec((m, LANES), lambda j: (0, 0))],
        out_shape=[jax.ShapeDtypeStruct((n, kdim), BF16), jax.ShapeDtypeStruct((LANES, kdim), BF16),
                   jax.ShapeDtypeStruct((m, n), F32), jax.ShapeDtypeStruct((m, LANES), F32)],
        compiler_params=_params(("arbitrary",)),
        name=name,
    )(x, wt_all, wt_all)


def _rope_rows_kernel(x_ref, cos_ref, sin_ref, o_ref, *, n_q):
    cos = cos_ref[...]
    sin = sin_ref[...]
    for g in range(x_ref.shape[1] // DH_DA):
        sl = slice(g * DH_DA, (g + 1) * DH_DA)
        y = _rotate(x_ref[:, sl], cos, sin)
        o_ref[:, sl] = y * ATTN_QSCALE if g < n_q else y


def _rope_rows(x, cos, sin):
    m, n = x.shape
    full = lambda c: pl.BlockSpec((m, c), lambda i: (0, 0))
    return pl.pallas_call(
        functools.partial(_rope_rows_kernel, n_q=D_ATTN // DH_DA),
        grid=(1,),
        in_specs=[full(n), full(DH_DA), full(DH_DA)],
        out_specs=full(n),
        out_shape=jax.ShapeDtypeStruct((m, n), F32),
        compiler_params=_params(("arbitrary",)),
        name="rope_sample",
    )(x, cos, sin)


def _attn_prompt_kernel(lq1, lk1, lq2, lk2, g_ref, q_ref, k_ref, v_ref, o_ref, *, t, nq, lam_init):
    qi = pl.program_id(2)
    q = q_ref[...]
    lam = _lambda(lq1, lk1, lq2, lk2, lam_init)
    tri = (lax.broadcasted_iota(jnp.int32, (t, t), 1) <= lax.broadcasted_iota(jnp.int32, (t, t), 0))

    def block(n_below):
        spans = ([(0, n_below * t, False)] if n_below else []) + [(n_below * t, (n_below + 1) * t, True)]
        probs, inv = [], []
        for c in range(2):
            sl = slice(c * DH_DA, (c + 1) * DH_DA)
            ss = []
            for lo, hi, diagonal in spans:
                s = lax.dot_general(q[:, sl], k_ref[lo:hi, sl], (((1,), (1,)), ((), ())),
                                    preferred_element_type=F32)
                ss.append(jnp.where(tri, s, NEG) if diagonal else s)
            m = functools.reduce(jnp.maximum, [jnp.max(s, axis=1, keepdims=True) for s in ss])
            ps = [jnp.exp2(s - m) for s in ss]
            probs.append(ps)
            inv.append(1.0 / functools.reduce(jnp.add, [jnp.sum(p, axis=1, keepdims=True) for p in ps]))
        r1 = inv[0]
        r2 = lam * inv[1]
        out = None
        for i, (lo, hi, _) in enumerate(spans):
            a = (probs[0][i] * r1 - probs[1][i] * r2).astype(BF16)
            d = jnp.dot(a, v_ref[lo:hi, :], preferred_element_type=F32)
            out = d if out is None else out + d
        ms = jnp.mean(out * out, axis=-1, keepdims=True)
        y = out * lax.rsqrt(ms + SUBLN_EPS) * g_ref[...]
        o_ref[...] = (y * (1.0 - lam_init)).astype(o_ref.dtype)

    for n_below in range(nq):
        pl.when(qi == n_below)(functools.partial(block, n_below))


def _attn_prompt(q, k, v, lam_rows, subln_g, lam_init, batch, seq, t=512):
    nq = seq // t
    dh2 = 2 * DH_DA
    row_spec = pl.BlockSpec((1, DH_DA), lambda b, h, i: (0, 0))
    return pl.pallas_call(
        functools.partial(_attn_prompt_kernel, t=t, nq=nq, lam_init=lam_init),
        grid=(batch, H_DA, nq),
        in_specs=[row_spec, row_spec, row_spec, row_spec,
                  pl.BlockSpec((1, dh2), lambda b, h, i: (0, 0)),
                  pl.BlockSpec((t, dh2), lambda b, h, i: (b * nq + i, h)),
                  pl.BlockSpec((seq, dh2), lambda b, h, i: (b, h)),
                  pl.BlockSpec((seq, dh2), lambda b, h, i: (b, h))],
        out_specs=pl.BlockSpec((t, dh2), lambda b, h, i: (b * nq + i, h)),
        out_shape=jax.ShapeDtypeStruct((batch * seq, D_ATTN), BF16),
        compiler_params=_params(("parallel", "parallel", "arbitrary")),
        name="attn_prompt",
    )(*lam_rows, subln_g.reshape(1, dh2), q, k, v)


def _attn_sample_kernel(*refs, lam_init, n_steps, group):
    pt_ref, lq1, lk1, lq2, lk2, g_ref, q_ref, kn_ref, vn_ref = refs[:9]
    k_refs = refs[9:9 + group]
    v_refs = refs[9 + group:9 + 2 * group]
    o_ref, m_sc, l_sc, acc_sc = refs[9 + 2 * group:]
    del pt_ref
    step = pl.program_id(1)
    rows = PAGE_SIZE * H_DA

    @pl.when(step == 0)
    def _():
        m_sc[...] = jnp.full(m_sc.shape, NEG, F32)
        l_sc[...] = jnp.zeros(l_sc.shape, F32)
        acc_sc[...] = jnp.zeros(acc_sc.shape, F32)

    q = q_ref[...]
    zero = jnp.zeros((H_DA, DH_DA), F32)
    qbd = jnp.concatenate([jnp.concatenate([q[:, :DH_DA], zero], axis=1),
                           jnp.concatenate([zero, q[:, DH_DA:]], axis=1)], axis=0).astype(BF16)
    own_head = (lax.broadcasted_iota(jnp.int32, (2 * H_DA, rows), 0) % H_DA
                == lax.broadcasted_iota(jnp.int32, (2 * H_DA, rows), 1) % H_DA)
    scores = []
    for g in range(group):
        k2 = k_refs[g][...].reshape(rows, 2 * DH_DA).astype(BF16)
        s = lax.dot_general(qbd, k2, (((1,), (1,)), ((), ())), preferred_element_type=F32)
        scores.append(jnp.where(own_head, s, NEG))
    m_prev = m_sc[...]
    m_new = m_prev
    for s in scores:
        m_new = jnp.maximum(m_new, jnp.max(s, axis=1, keepdims=True))
    alpha = jnp.exp2(m_prev - m_new)
    l_new = alpha * l_sc[...]
    acc = alpha * acc_sc[...]
    for g in range(group):
        p = jnp.exp2(scores[g] - m_new)
        l_new = l_new + jnp.sum(p, axis=1, keepdims=True)
        v2 = v_refs[g][...].reshape(rows, 2 * DH_DA).astype(BF16)
        acc = acc + jnp.dot(p.astype(BF16), v2, preferred_element_type=F32)
    m_sc[...] = m_new
    l_sc[...] = l_new
    acc_sc[...] = acc

    @pl.when(step == n_steps - 1)
    def _():
        kn = kn_ref[...]
        vn = vn_ref[...]
        pn = kn * q
        sn = jnp.concatenate([jnp.sum(pn[:, :DH_DA], axis=-1, keepdims=True),
                              jnp.sum(pn[:, DH_DA:], axis=-1, keepdims=True)], axis=0)
        m_fin = jnp.maximum(m_new, sn)
        pe = jnp.exp2(sn - m_fin)
        a_fin = jnp.exp2(m_new - m_fin)
        l_fin = a_fin * l_new + pe
        outs = (a_fin * acc + pe * jnp.concatenate([vn, vn], axis=0)) / l_fin
        lam = _lambda(lq1, lk1, lq2, lk2, lam_init)
        out = outs[0:H_DA] - lam * outs[H_DA:]
        ms = jnp.mean(out * out, axis=-1, keepdims=True)
        y = out * lax.rsqrt(ms + SUBLN_EPS) * g_ref[...]
        o_ref[...] = (y * (1.0 - lam_init)).astype(o_ref.dtype)


def _attn_sample(page_table, q, k_new, v_new, cache_k, cache_v, layer, lam_rows, subln_g, lam_init, group=8):
    bd, n_pages = page_table.shape
    n_steps = n_pages // group
    dh2 = 2 * DH_DA
    row_spec = pl.BlockSpec((1, DH_DA), lambda b, p, pt: (0, 0))
    tok_spec = pl.BlockSpec((None, H_DA, dh2), lambda b, p, pt: (b, 0, 0))

    def page_spec(g):
        return pl.BlockSpec((None, None, PAGE_SIZE, H_DA, dh2),
                            lambda b, p, pt: (layer, pt[b, p * group + g], 0, 0, 0))

    pages = [page_spec(g) for g in range(group)]
    grid_spec = pltpu.PrefetchScalarGridSpec(
        num_scalar_prefetch=1,
        grid=(bd, n_steps),
        in_specs=[row_spec, row_spec, row_spec, row_spec,
                  pl.BlockSpec((1, dh2), lambda b, p, pt: (0, 0)),
                  tok_spec, tok_spec, tok_spec] + pages + pages,
        out_specs=tok_spec,
        scratch_shapes=[pltpu.VMEM((2 * H_DA, 1), F32), pltpu.VMEM((2 * H_DA, 1), F32),
                        pltpu.VMEM((2 * H_DA, dh2), F32)],
    )
    return pl.pallas_call(
        functools.partial(_attn_sample_kernel, lam_init=lam_init, n_steps=n_steps, group=group),
        grid_spec=grid_spec,
        out_shape=jax.ShapeDtypeStruct((bd, H_DA, dh2), F32),
        compiler_params=_params(("parallel", "arbitrary")),
        name="attn_sample",
    )(page_table, *lam_rows, subln_g.reshape(1, dh2), q, k_new, v_new,
      *([cache_k] * group), *([cache_v] * group))


def _mlstm_out(h, g, om):
    mu = jnp.mean(h, axis=-1, keepdims=True)
    hc = h - mu
    var = jnp.mean(hc * hc, axis=-1, keepdims=True)
    return hc * lax.rsqrt(var + NORM_EPS) * g * _sigmoid(om)


def _mlstm_prompt_kernel(big_ref, bfg_ref, q_ref, k_ref, v_ref, om_ref, gc_ref, gr_ref, ng_ref,
                         hm_ref, c_out, n_out, m_out, c_sc, n_sc, m_sc, *, n_chunks):
    ci = pl.program_id(1)
    L = MLSTM_CHUNK

    @pl.when(ci == 0)
    def _():
        c_sc[...] = jnp.zeros(c_sc.shape, F32)
        n_sc[...] = jnp.zeros(n_sc.shape, F32)
        m_sc[...] = jnp.zeros(m_sc.shape, F32)

    ri = lax.broadcasted_iota(jnp.int32, (L, L), 0)
    cj = lax.broadcasted_iota(jnp.int32, (L, L), 1)
    lower = ri >= cj
    gc = gc_ref[...]
    gr = gr_ref[...]
    for h in range(H_M):
        hs = slice(h * DH_M, (h + 1) * DH_M)
        ig_c = gc[:, h:h + 1] + big_ref[h]
        lf_c = _log_sigmoid(gc[:, H_M + h:H_M + h + 1] + bfg_ref[h])
        ig_r = gr[h:h + 1, :] + big_ref[h]
        lf_r = _log_sigmoid(gr[H_M + h:H_M + h + 1, :] + bfg_ref[h])
        b_col = jnp.sum(jnp.where(lower, lf_r, 0.0), axis=1, keepdims=True)
        b_row = jnp.sum(jnp.where(cj >= ri, lf_c, 0.0), axis=0, keepdims=True)
        dm = jnp.where(lower, b_col - b_row + ig_r, -jnp.inf)
        m_prev = m_sc[h:h + 1, 0:1]
        inter = b_col + m_prev
        m_t = jnp.maximum(inter, jnp.max(dm, axis=1, keepdims=True))
        w_intra = jnp.exp(dm - m_t)
        w_inter = jnp.exp(inter - m_t)
        q = q_ref[:, hs]
        ks = k_ref[:, hs] * (DH_M ** -0.5)
        v = v_ref[:, hs]
        c_prev = c_sc[h]
        n_prev = n_sc[h:h + 1, :]
        sc = w_intra * lax.dot_general(q, ks, (((1,), (1,)), ((), ())), preferred_element_type=F32)
        num = (jnp.dot(sc.astype(BF16), v, preferred_element_type=F32)
               + w_inter * lax.dot_general(q, c_prev.astype(BF16), (((1,), (1,)), ((), ())),
                                           preferred_element_type=F32))
        den = (jnp.sum(sc, axis=1, keepdims=True)
               + w_inter * jnp.sum(q.astype(F32) * n_prev, axis=1, keepdims=True))
        hh = num / jnp.maximum(jnp.abs(den), jnp.exp(-m_t))
        m_new = m_t[L - 1:L, :]
        decay = w_inter[L - 1:L, :]
        wk = jnp.exp(b_col[L - 1:L, :] - b_col + ig_c - m_new)
        wv = (wk * v.astype(F32)).astype(BF16)
        c_sc[h] = decay * c_prev + lax.dot_general(wv, ks, (((0,), (0,)), ((), ())),
                                                   preferred_element_type=F32)
        n_sc[h:h + 1, :] = decay * n_prev + jnp.sum(wk * ks.astype(F32), axis=0, keepdims=True)
        m_sc[h:h + 1, :] = jnp.broadcast_to(m_new, (1, LANES))
        hm_ref[:, hs] = _mlstm_out(hh, ng_ref[:, hs], om_ref[:, hs].astype(F32)).astype(hm_ref.dtype)

    @pl.when(ci == n_chunks - 1)
    def _():
        c_out[...] = c_sc[...]
        n_out[...] = n_sc[0:H_M, :]
        m_out[...] = m_sc[0:H_M, :]


def _mlstm_prompt(um, gates, gates_t, b_ig, b_fg, norm_g, batch, seq):
    L = MLSTM_CHUNK
    nc = seq // L
    smem = pl.BlockSpec(memory_space=pltpu.SMEM)

    def col(cb):
        return pl.BlockSpec((L, D_MLSTM), lambda b, c: (b * nc + c, cb))

    return pl.pallas_call(
        functools.partial(_mlstm_prompt_kernel, n_chunks=nc),
        grid=(batch, nc),
        in_specs=[smem, smem, col(0), col(1), col(2), col(3),
                  pl.BlockSpec((L, LANES), lambda b, c: (b * nc + c, 0)),
                  pl.BlockSpec((None, SUBLANES, L), lambda b, c: (b, 0, c)),
                  pl.BlockSpec((1, D_MLSTM), lambda b, c: (0, 0))],
        out_specs=[pl.BlockSpec((L, D_MLSTM), lambda b, c: (b * nc + c, 0)),
                   pl.BlockSpec((None, H_M, DH_M, DH_M), lambda b, c: (b, 0, 0, 0)),
                   pl.BlockSpec((None, H_M, DH_M), lambda b, c: (b, 0, 0)),
                   pl.BlockSpec((None, H_M, LANES), lambda b, c: (b, 0, 0))],
        out_shape=[jax.ShapeDtypeStruct((batch * seq, D_MLSTM), BF16),
                   jax.ShapeDtypeStruct((batch, H_M, DH_M, DH_M), F32),
                   jax.ShapeDtypeStruct((batch, H_M, DH_M), F32),
                   jax.ShapeDtypeStruct((batch, H_M, LANES), F32)],
        scratch_shapes=[pltpu.VMEM((H_M, DH_M, DH_M), F32), pltpu.VMEM((SUBLANES, DH_M), F32),
                        pltpu.VMEM((SUBLANES, LANES), F32)],
        compiler_params=_params(("parallel", "arbitrary")),
        name="mlstm_prompt",
    )(b_ig, b_fg, um, um, um, um, gates, gates_t, norm_g.reshape(1, D_MLSTM))


def _mlstm_sample_kernel(big_ref, bfg_ref, u_ref, vcol_ref, g_ref, ng_ref, c_ref, n_ref, m_ref,
                         hm_ref, c_out, n_out, m_out):
    u = u_ref[...]
    g = g_ref[...]
    for h in range(H_M):
        def part(i):
            return u[:, i * D_MLSTM + h * DH_M:i * D_MLSTM + (h + 1) * DH_M]
        q, k, v, om = part(0), part(1), part(2), part(3)
        ks = k * (DH_M ** -0.5)
        ig = g[:, h:h + 1] + big_ref[h]
        lf = _log_sigmoid(g[:, H_M + h:H_M + h + 1] + bfg_ref[h])
        m_prev = m_ref[:, h:h + 1]
        inter = lf + m_prev
        m_t = jnp.maximum(inter, ig)
        w_intra = jnp.exp(ig - m_t)
        w_inter = jnp.exp(inter - m_t)
        c_prev = c_ref[h]
        n_prev = n_ref[h:h + 1, :]
        q8 = jnp.broadcast_to(q, (2 * SUBLANES, DH_M)).astype(BF16)
        cq = lax.dot_general(q8, c_prev.astype(BF16), (((1,), (1,)), ((), ())),
                             preferred_element_type=F32)[0:1, :]
        sc = w_intra * jnp.sum(q * ks, axis=-1, keepdims=True)
        num = sc * v + w_inter * cq
        den = sc + w_inter * jnp.sum(n_prev * q, axis=-1, keepdims=True)
        hh = num / jnp.maximum(jnp.abs(den), jnp.exp(-m_t))
        c_out[h] = w_inter * c_prev + (w_intra * vcol_ref[h]) * ks
        n_out[h:h + 1, :] = w_inter * n_prev + w_intra * ks
        m_out[h:h + 1, :] = jnp.broadcast_to(m_t, (1, LANES))
        hs = slice(h * DH_M, (h + 1) * DH_M)
        hm_ref[:, hs] = _mlstm_out(hh, ng_ref[:, hs], om)


def _mlstm_sample(um, gates, b_ig, b_fg, norm_g, c_state, n_state, m_state):
    bd = um.shape[0]
    smem = pl.BlockSpec(memory_space=pltpu.SMEM)
    u4 = um.reshape(bd, 1, 4 * D_MLSTM)
    vcol = um[:, 2 * D_MLSTM:3 * D_MLSTM].reshape(bd, H_M, DH_M, 1)
    return pl.pallas_call(
        _mlstm_sample_kernel,
        grid=(bd,),
        in_specs=[smem, smem,
                  pl.BlockSpec((None, 1, 4 * D_MLSTM), lambda b: (b, 0, 0)),
                  pl.BlockSpec((None, H_M, DH_M, 1), lambda b: (b, 0, 0, 0)),
                  pl.BlockSpec((None, 1, LANES), lambda b: (b, 0, 0)),
                  pl.BlockSpec((1, D_MLSTM), lambda b: (0, 0)),
                  pl.BlockSpec((None, H_M, DH_M, DH_M), lambda b: (b, 0, 0, 0)),
                  pl.BlockSpec((None, H_M, DH_M), lambda b: (b, 0, 0)),
                  pl.BlockSpec((None, 1, H_M), lambda b: (b, 0, 0))],
        out_specs=[pl.BlockSpec((None, 1, D_MLSTM), lambda b: (b, 0, 0)),
                   pl.BlockSpec((None, H_M, DH_M, DH_M), lambda b: (b, 0, 0, 0)),
                   pl.BlockSpec((None, H_M, DH_M), lambda b: (b, 0, 0)),
                   pl.BlockSpec((None, H_M, LANES), lambda b: (b, 0, 0))],
        out_shape=[jax.ShapeDtypeStruct((bd, 1, D_MLSTM), F32),
                   jax.ShapeDtypeStruct((bd, H_M, DH_M, DH_M), F32),
                   jax.ShapeDtypeStruct((bd, H_M, DH_M), F32),
                   jax.ShapeDtypeStruct((bd, H_M, LANES), F32)],
        compiler_params=_params(("parallel",)),
        name="mlstm_sample",
    )(b_ig, b_fg, u4, vcol, gates.reshape(bd, 1, LANES), norm_g.reshape(1, D_MLSTM),
      c_state, n_state, m_state.reshape(bd, 1, H_M))


def _rglru_coeffs(xc, wra_ref, bra_ref, wrx_ref, brx_ref, lam_ref):
    rs, is_ = [], []
    for n in range(RG_BLOCKS):
        xb = xc[:, n * RG_BW:(n + 1) * RG_BW].astype(BF16)
        rs.append(jnp.dot(xb, wra_ref[n], preferred_element_type=F32))
        is_.append(jnp.dot(xb, wrx_ref[n], preferred_element_type=F32))
    r = _sigmoid(jnp.concatenate(rs, axis=-1) + bra_ref[...])
    i = _sigmoid(jnp.concatenate(is_, axis=-1) + brx_ref[...])
    log_a = -RG_C * r * _softplus(-lam_ref[...])
    a = jnp.exp(log_a)
    mult = jnp.sqrt(1.0 - a * a)
    return a, mult * i * xc


def _rglru_prompt_kernel(x_ref, gate_ref, cw_ref, cb_ref, wra_ref, bra_ref, wrx_ref, brx_ref, lam_ref,
                         y_ref, conv_out, h_out, xbuf, a_sc, b_sc, h_sc, *, tt, n_tiles):
    ti = pl.program_id(1)
    pad = SUBLANES

    @pl.when(ti == 0)
    def _():
        xbuf[0:pad, :] = jnp.zeros((pad, D_RG), F32)
        h_sc[...] = jnp.zeros(h_sc.shape, F32)

    xbuf[pad:pad + tt, :] = x_ref[...].astype(F32)
    xc = cb_ref[...] + sum(xbuf[pad - (CONV_W - 1) + j:pad - (CONV_W - 1) + j + tt, :] * cw_ref[j:j + 1, :]
                           for j in range(CONV_W))
    a, b = _rglru_coeffs(xc, wra_ref, bra_ref, wrx_ref, brx_ref, lam_ref)
    a_sc[...] = a
    b_sc[...] = b

    def step(t, h):
        h = a_sc[pl.ds(t, 1), :] * h + b_sc[pl.ds(t, 1), :]
        b_sc[pl.ds(t, 1), :] = h
        return h

    h_last = lax.fori_loop(0, tt, step, h_sc[...], unroll=8)
    h_sc[...] = h_last
    y_ref[...] = (b_sc[...] * _gelu_tanh(gate_ref[...].astype(F32))).astype(y_ref.dtype)
    tail = xbuf[tt:tt + pad, :]
    xbuf[0:pad, :] = tail

    @pl.when(ti == n_tiles - 1)
    def _():
        conv_out[...] = tail[pad - (CONV_W - 1):, :]
        h_out[...] = h_last


def _rglru_weights(conv_w, conv_b, w_ra, b_ra, w_rx, b_rx, lam):
    row = lambda z: z.reshape(1, D_RG).astype(F32)
    return (conv_w.astype(F32), row(conv_b), w_ra.astype(BF16), row(b_ra), w_rx.astype(BF16), row(b_rx), row(lam))


def _rglru_weight_specs():
    zeros2 = (lambda *a: (0, 0))
    zeros3 = (lambda *a: (0, 0, 0))
    return [pl.BlockSpec((CONV_W, D_RG), zeros2), pl.BlockSpec((1, D_RG), zeros2),
            pl.BlockSpec((RG_BLOCKS, RG_BW, RG_BW), zeros3), pl.BlockSpec((1, D_RG), zeros2),
            pl.BlockSpec((RG_BLOCKS, RG_BW, RG_BW), zeros3), pl.BlockSpec((1, D_RG), zeros2),
            pl.BlockSpec((1, D_RG), zeros2)]


def _rglru_prompt(urg, weights, batch, seq, tt=256):
    nt = seq // tt
    return pl.pallas_call(
        functools.partial(_rglru_prompt_kernel, tt=tt, n_tiles=nt),
        grid=(batch, nt),
        in_specs=[pl.BlockSpec((tt, D_RG), lambda b, t: (b * nt + t, 0)),
                  pl.BlockSpec((tt, D_RG), lambda b, t: (b * nt + t, 1))] + _rglru_weight_specs(),
        out_specs=[pl.BlockSpec((tt, D_RG), lambda b, t: (b * nt + t, 0)),
                   pl.BlockSpec((None, CONV_W - 1, D_RG), lambda b, t: (b, 0, 0)),
                   pl.BlockSpec((None, 1, D_RG), lambda b, t: (b, 0, 0))],
        out_shape=[jax.ShapeDtypeStruct((batch * seq, D_RG), BF16),
                   jax.ShapeDtypeStruct((batch, CONV_W - 1, D_RG), F32),
                   jax.ShapeDtypeStruct((batch, 1, D_RG), F32)],
        scratch_shapes=[pltpu.VMEM((tt + SUBLANES, D_RG), F32), pltpu.VMEM((tt, D_RG), F32),
                        pltpu.VMEM((tt, D_RG), F32), pltpu.VMEM((1, D_RG), F32)],
        compiler_params=_params(("parallel", "arbitrary")),
        name="rglru_prompt",
    )(urg, urg, *weights)


def _rglru_sample_kernel(x_ref, gate_ref, p0_ref, p1_ref, p2_ref, h0_ref, cw_ref, cb_ref, wra_ref, bra_ref,
                         wrx_ref, brx_ref, lam_ref, y_ref, h_out):
    x = x_ref[...]
    xc = cb_ref[...] + (p0_ref[...] * cw_ref[0:1, :] + p1_ref[...] * cw_ref[1:2, :]
                        + p2_ref[...] * cw_ref[2:3, :] + x * cw_ref[3:4, :])
    a, b = _rglru_coeffs(xc, wra_ref, bra_ref, wrx_ref, brx_ref, lam_ref)
    h = a * h0_ref[...] + b
    h_out[...] = h
    y_ref[...] = h * _gelu_tanh(gate_ref[...])


def _rglru_sample(x, gate, conv_prev, h0, weights):
    bd = x.shape[0]
    full = pl.BlockSpec((bd, D_RG), lambda i: (0, 0))
    return pl.pallas_call(
        _rglru_sample_kernel,
        grid=(1,),
        in_specs=[full] * 6 + _rglru_weight_specs(),
        out_specs=[full, full],
        out_shape=[jax.ShapeDtypeStruct((bd, D_RG), F32)] * 2,
        compiler_params=_params(("arbitrary",)),
        name="rglru_sample",
    )(x, gate, conv_prev[:, 0], conv_prev[:, 1], conv_prev[:, 2], h0, *weights)


def _rope_tables(pos):
    inv = 1.0 / (ROPE_THETA ** (jnp.arange(0, DH_DA, 2, dtype=F32) / DH_DA))
    ang = pos.astype(F32)[:, None] * inv[None, :]
    ang = jnp.concatenate([ang, ang], axis=-1)
    sign = jnp.concatenate([-jnp.ones((DH_DA // 2,), F32), jnp.ones((DH_DA // 2,), F32)])
    return jnp.cos(ang), jnp.sin(ang) * sign[None, :]


def kernel(x_prompt, x_sample, cache_k, cache_v, page_table, state_mlstm_c, state_mlstm_n, state_mlstm_m, state_conv, state_rglru_h, norm_mix_g, w_in, w_out, lam_q1, lam_k1, lam_q2, lam_k2, attn_subln_g, b_ig, b_fg, mlstm_norm_g, conv_w, conv_b, w_ra, b_ra, w_rx, b_rx, rg_lambda, norm_mlp_g, w_up, w_down, final_norm_g):
    bp, sp, _ = x_prompt.shape
    bd, td, _ = x_sample.shape
    assert td == 1
    past = page_table.shape[1] * PAGE_SIZE
    mp = bp * sp
    dh2 = 2 * DH_DA
    cos_p, sin_p = _rope_tables(jnp.arange(sp))
    cos_s, sin_s = _rope_tables(jnp.full((bd,), past))

    TM = 1024
    TNC = 512

    xp = x_prompt.reshape(mp, D_MODEL)
    xs = x_sample.reshape(bd, D_MODEL)
    w_in_t = jnp.swapaxes(w_in, 1, 2)
    k_all = v_all = None
    outs = {n: [] for n in ("ks", "vs", "cp", "np", "mp", "cs", "ns", "ms", "cvp", "cvs", "hp", "hs")}
    for l in range(DEPTH):
        lam_init = 0.8 - 0.6 * math.exp(-0.3 * l)
        lam_rows = [z[l].reshape(1, DH_DA).astype(F32) for z in (lam_q1, lam_k1, lam_q2, lam_k2)]
        rg_w = _rglru_weights(conv_w[l], conv_b[l], w_ra[l], b_ra[l], w_rx[l], b_rx[l], rg_lambda[l])

        hs_n = _rmsnorm(xs, norm_mix_g[l], NORM_EPS, F32, tm=bd)
        u_s, w_in_b = _cast_matmul_t(hs_n, w_in_t, l, D_MAIN, tn=TNC, name="s_proj_in")
        wr, wg, urg_s, gates_s = _cast_tail(hs_n, w_in_t, l, tn=TNC, name="s_proj_tail")
        qk_s = _rope_rows(u_s[:, :2 * D_ATTN], cos_s, sin_s)
        q_s, k_s = qk_s[:, :D_ATTN], qk_s[:, D_ATTN:]
        v_s = u_s[:, 2 * D_ATTN:3 * D_ATTN]
        um_s = u_s[:, 3 * D_ATTN:]

        hn = _rmsnorm(xp, norm_mix_g[l], NORM_EPS, BF16, tm=512)
        mm = functools.partial(_matmul, hn, tm=TM, tk=D_MODEL, w_t=True)
        prev_k = None if k_all is None else [k_all, None]
        prev_v = None if v_all is None else [v_all, None]
        q = mm(w_in_b, n=D_ATTN, w_col0=0, tn=512, out_dtypes=(BF16,), rope=(cos_p, sin_p),
               rope_scale=ATTN_QSCALE, name="proj_q")
        k_all, k16 = mm(w_in_b, n=D_ATTN, w_col0=D_ATTN, tn=512, out_dtypes=(F32, BF16), rope=(cos_p, sin_p),
                        stacked=[True, False], prev=prev_k, name="proj_k")
        v_all, v16 = mm(w_in_b, n=D_ATTN, w_col0=2 * D_ATTN, tn=512, out_dtypes=(F32, BF16),
                        stacked=[True, False], prev=prev_v, name="proj_v")
        um = mm(w_in_b, n=4 * D_MLSTM, w_col0=3 * D_ATTN, tn=1024, out_dtypes=(BF16,), name="proj_m")
        urg = mm(wr, tn=1024, out_dtypes=(BF16,), name="proj_r")
        gates = mm(wg, tn=LANES, out_dtypes=(F32,), name="proj_g")
        gates_t = gates[:, :SUBLANES].reshape(bp, sp, SUBLANES).transpose(0, 2, 1)
        att = _attn_prompt(q, k16, v16, lam_rows, attn_subln_g[l], lam_init, bp, sp)
        hm, c_p, n_p, m_p = _mlstm_prompt(um, gates, gates_t, b_ig[l], b_fg[l], mlstm_norm_g[l], bp, sp)
        hr, cv_p, h_p = _rglru_prompt(urg, rg_w, bp, sp)

        att_s = _attn_sample(page_table, q_s.reshape(bd, H_DA, dh2), k_s.reshape(bd, H_DA, dh2),
                             v_s.reshape(bd, H_DA, dh2), cache_k, cache_v, l, lam_rows,
                             attn_subln_g[l], lam_init)
        hm_s, c_s, n_s, m_s = _mlstm_sample(um_s, gates_s, b_ig[l], b_fg[l], mlstm_norm_g[l],
                                            state_mlstm_c[l], state_mlstm_n[l], state_mlstm_m[l])
        x_rg = urg_s[:, :D_RG]
        hr_s, h_s = _rglru_sample(x_rg, urg_s[:, D_RG:], state_conv[l], state_rglru_h[l], rg_w)
        cat_s = jnp.concatenate([att_s.reshape(bd, D_ATTN), hm_s.reshape(bd, D_MLSTM), hr_s], axis=1)
        x1s, wo = _cast_matmul(cat_s, w_out, l, D_MODEL, tn=TNC, tk=D_MODEL, residual=xs, name="s_proj_out")
        hs_n2 = _rmsnorm(x1s, norm_mlp_g[l], NORM_EPS, F32, tm=bd)
        act_s, wu = _cast_matmul(hs_n2, w_up, l, D_FF, tn=TNC, tk=D_MODEL, act="relu2", name="s_mlp_up")
        xs, wd = _cast_matmul(act_s, w_down, l, D_MODEL, tn=TNC, tk=D_MODEL, residual=x1s, name="s_mlp_down")

        x1 = _matmul([att, hm, hr], wo, tm=TM, tn=512, tk=D_MODEL, out_dtypes=(F32,), residual=xp,
                     name="proj_out")
        hn2 = _rmsnorm(x1, norm_mlp_g[l], NORM_EPS, BF16, tm=512)
        act = _matmul(hn2, wu, tm=TM, tn=1024, tk=D_MODEL, out_dtypes=(BF16,), act="relu2", name="mlp_up")
        xp = _matmul(act, wd, tm=512, tn=256, tk=D_FF, out_dtypes=(F32,), residual=x1, name="mlp_down")

        outs["ks"].append(k_s.reshape(bd, td, H_DA, dh2))
        outs["vs"].append(v_s.reshape(bd, td, H_DA, dh2))
        outs["cp"].append(c_p)
        outs["np"].append(n_p)
        outs["mp"].append(m_p[:, :, 0])
        outs["cs"].append(c_s)
        outs["ns"].append(n_s)
        outs["ms"].append(m_s[:, :, 0])
        outs["cvp"].append(cv_p)
        outs["cvs"].append(jnp.concatenate([state_conv[l][:, 1:], x_rg[:, None, :]], axis=1))
        outs["hp"].append(h_p.reshape(bp, D_RG))
        outs["hs"].append(h_s)

    y_prompt = _rmsnorm(xp, final_norm_g, NORM_EPS, F32, tm=512).reshape(bp, sp, D_MODEL)
    y_sample = _rmsnorm(xs, final_norm_g, NORM_EPS, F32, tm=bd).reshape(bd, td, D_MODEL)
    st = lambda n: jnp.stack(outs[n])
    return (y_prompt, y_sample, k_all.reshape(DEPTH, bp, sp, H_DA, dh2), v_all.reshape(DEPTH, bp, sp, H_DA, dh2),
            st("ks"), st("vs"), st("cp"), st("np"), st("mp"),
            st("cs"), st("ns"), st("ms"), st("cvp"), st("cvs"), st("hp"), st("hs"))
```

```python
import functools
import math

import jax
import jax.numpy as jnp
from jax import lax
from jax.experimental import pallas as pl
from jax.experimental.pallas import tpu as pltpu

D_MODEL = 4096
DEPTH = 2
PAGE_SIZE = 128
D_ATTN = D_MODEL // 2
D_MLSTM = D_MODEL // 4
D_RG = D_MODEL - D_ATTN - D_MLSTM
DH_DA = 128
H_DA = D_ATTN // (2 * DH_DA)
H_M = 4
DH_M = D_MLSTM // H_M
RG_BLOCKS = 8
RG_BW = D_RG // RG_BLOCKS
CONV_W = 4
RG_C = 8.0
D_FF = 4 * D_MODEL
ROPE_THETA = 10000.0
MLSTM_CHUNK = 128
NORM_EPS = 1e-6
SUBLN_EPS = 1e-5
NEG = -1e30
D_MAIN = 3 * D_ATTN + 4 * D_MLSTM
ATTN_QSCALE = DH_DA ** -0.5 * math.log2(math.e)

MXU_COLS = 256
LANES = 128
SUBLANES = 8
VMEM_LIMIT = 56 * 1024 * 1024

F32 = jnp.float32
BF16 = jnp.bfloat16


def _params(sem):
    return pltpu.CompilerParams(dimension_semantics=sem, vmem_limit_bytes=VMEM_LIMIT)


def _sigmoid(x):
    return 1.0 / (1.0 + jnp.exp(-x))


def _softplus(x):
    return jnp.maximum(x, 0.0) + jnp.log1p(jnp.exp(-jnp.abs(x)))


def _log_sigmoid(x):
    return -_softplus(-x)


def _gelu_tanh(x):
    c = math.sqrt(2.0 / math.pi)
    return 0.5 * x * (1.0 + jnp.tanh(c * (x + 0.044715 * (x * x * x))))


def _lambda(lq1, lk1, lq2, lk2, lam_init):
    a = jnp.exp(jnp.sum(lq1[...] * lk1[...], axis=-1, keepdims=True))
    b = jnp.exp(jnp.sum(lq2[...] * lk2[...], axis=-1, keepdims=True))
    return a - b + lam_init


def _rotate(x, cos, sin_signed):
    return x * cos + pltpu.roll(x, DH_DA // 2, axis=1) * sin_signed


def _rmsnorm_kernel(x_ref, g_ref, o_ref, *, eps):
    x = x_ref[...].astype(F32)
    ms = jnp.mean(x * x, axis=-1, keepdims=True)
    o_ref[...] = (x * lax.rsqrt(ms + eps) * g_ref[...]).astype(o_ref.dtype)


def _rmsnorm(x, g, eps, out_dtype, tm):
    m, d = x.shape
    return pl.pallas_call(
        functools.partial(_rmsnorm_kernel, eps=eps),
        grid=(m // tm,),
        in_specs=[pl.BlockSpec((tm, d), lambda i: (i, 0)),
                  pl.BlockSpec((1, d), lambda i: (0, 0))],
        out_specs=pl.BlockSpec((tm, d), lambda i: (i, 0)),
        out_shape=jax.ShapeDtypeStruct((m, d), out_dtype),
        compiler_params=_params(("parallel",)),
        name="rmsnorm",
    )(x, g.reshape(1, d).astype(F32))


def _mm_kernel(*refs, nk, seg, w_t, out_depths, n_prev, has_res, has_rope, rope_scale, act, tn,
               has_scale, norm_next):
    it = iter(refs)
    x_refs = [next(it) for _ in seg]
    w_ref = next(it)
    cos_ref = next(it) if has_rope else None
    sin_ref = next(it) if has_rope else None
    res_ref = next(it) if has_res else None
    ss_ref = next(it) if has_scale else None
    g_ref = next(it) if norm_next else None
    prev_refs = [next(it) for _ in range(n_prev)]
    o_refs = [next(it) for _ in out_depths]
    xg_ref = next(it) if norm_next else None
    ss_out = next(it) if norm_next else None

    def emit_norm_inputs(x_new):
        xg_ref[...] = (x_new * g_ref[...]).astype(xg_ref.dtype)
        rows = jnp.broadcast_to(jnp.sum(x_new * x_new, axis=1, keepdims=True), ss_out.shape)
        j = pl.program_id(1)

        @pl.when(j == 0)
        def _():
            ss_out[...] = rows

        @pl.when(j > 0)
        def _():
            ss_out[...] += rows

    if nk > 1:
        o = o_refs[0]
        k = pl.program_id(2)
        chunks = [slice(c, c + MXU_COLS) for c in range(0, tn, MXU_COLS)]

        @pl.when(k == 0)
        def _():
            for sl in chunks:
                d = jnp.dot(x_refs[0][...], w_ref[:, sl], preferred_element_type=F32)
                o[:, sl] = res_ref[:, sl] + d if has_res else d

        @pl.when(k > 0)
        def _():
            for sl in chunks:
                o[:, sl] += jnp.dot(x_refs[0][...], w_ref[:, sl], preferred_element_type=F32)

        if norm_next:
            pl.when(k == nk - 1)(lambda: emit_norm_inputs(o[...]))
        return

    if w_t:
        part = lax.dot_general(x_refs[0][...], w_ref[...], (((1,), (1,)), ((), ())),
                               preferred_element_type=F32)
    elif len(seg) == 1:
        part = jnp.dot(x_refs[0][...].astype(BF16), w_ref[...], preferred_element_type=F32)
    else:
        part, off = None, 0
        for x_ref, width in zip(x_refs, seg):
            d = jnp.dot(x_ref[...], w_ref[off:off + width, :], preferred_element_type=F32)
            part = d if part is None else part + d
            off += width

    def store(sl, val):
        for o, depth in zip(o_refs, out_depths):
            if depth is None:
                o[:, sl] = val.astype(o.dtype)
            else:
                o[depth - 1, :, sl] = val.astype(o.dtype)

    def epilogue(acc):
        pi = 0
        for o, depth in zip(o_refs, out_depths):
            if depth is not None and depth > 1:
                o[0:depth - 1] = prev_refs[pi][...]
                pi += 1
        if has_scale:
            acc = acc * lax.rsqrt(ss_ref[:, 0:1] * (1.0 / D_MODEL) + NORM_EPS)
        if has_rope:
            cos = cos_ref[...]
            sin = sin_ref[...]
            for g in range(tn // DH_DA):
                sl = slice(g * DH_DA, (g + 1) * DH_DA)
                store(sl, _rotate(acc[:, sl], cos, sin) * rope_scale)
            return
        if act == "relu2":
            r = jnp.maximum(acc, 0.0)
            acc = r * r
        if has_res:
            acc = res_ref[...] + acc
        store(slice(None), acc)
        if norm_next:
            emit_norm_inputs(acc)

    epilogue(part)


def _matmul(xs, w, *, tm, tn, tk, out_dtypes, n=None, w_col0=0, w_t=False, residual=None, rope=None,
            rope_scale=1.0, act=None, stacked=None, prev=None, row_ss=None, next_gain=None, name="matmul"):
    xs = list(xs) if isinstance(xs, (list, tuple)) else [xs]
    seg = [x.shape[1] for x in xs]
    m = xs[0].shape[0]
    kdim = sum(seg)
    n = w.shape[0 if w_t else 1] - w_col0 if n is None else n
    nk = kdim // tk
    assert len(xs) == 1 or (nk == 1 and not w_t)
    if nk > 1:
        assert (not w_t and rope is None and act is None and row_ss is None and not stacked
                and tuple(out_dtypes) == (F32,) and xs[0].dtype == BF16)
    jb = w_col0 // tn
    stacked = stacked or [False] * len(out_dtypes)
    prev = prev or [None] * len(out_dtypes)

    if len(xs) == 1:
        in_specs = [pl.BlockSpec((tm, tk), lambda i, j, k: (i, k))]
    else:
        in_specs = [pl.BlockSpec((tm, s), lambda i, j, k: (i, 0)) for s in seg]
    if w_t:
        in_specs.append(pl.BlockSpec((tn, tk), lambda i, j, k: (j + jb, k)))
    else:
        in_specs.append(pl.BlockSpec((tk, tn), lambda i, j, k: (k, j + jb)))
    args = xs + [w]
    if rope is not None:
        nr = rope[0].shape[0] // tm
        spec = pl.BlockSpec((tm, DH_DA), lambda i, j, k: (i % nr, 0))
        in_specs += [spec, spec]
        args += [rope[0], rope[1]]
    if residual is not None:
        in_specs.append(pl.BlockSpec((tm, tn), lambda i, j, k: (i, j)))
        args.append(residual)
    if row_ss is not None:
        in_specs.append(pl.BlockSpec((tm, LANES), lambda i, j, k: (i, 0)))
        args.append(row_ss)
    if next_gain is not None:
        in_specs.append(pl.BlockSpec((1, tn), lambda i, j, k: (0, j)))
        args.append(next_gain.reshape(1, n).astype(F32))
    out_depths, out_specs, out_shapes, n_prev = [], [], [], 0
    for dt, st, pv in zip(out_dtypes, stacked, prev):
        if not st:
            out_depths.append(None)
            out_specs.append(pl.BlockSpec((tm, tn), lambda i, j, k: (i, j)))
            out_shapes.append(jax.ShapeDtypeStruct((m, n), dt))
            continue
        depth = 1 if pv is None else pv.shape[0] + 1
        out_depths.append(depth)
        out_specs.append(pl.BlockSpec((depth, tm, tn), lambda i, j, k: (0, i, j)))
        out_shapes.append(jax.ShapeDtypeStruct((depth, m, n), dt))
        if pv is not None:
            in_specs.append(pl.BlockSpec((depth - 1, tm, tn), lambda i, j, k: (0, i, j)))
            args.append(pv)
            n_prev += 1
    if next_gain is not None:
        out_specs += [pl.BlockSpec((tm, tn), lambda i, j, k: (i, j)),
                      pl.BlockSpec((tm, LANES), lambda i, j, k: (i, 0))]
        out_shapes += [jax.ShapeDtypeStruct((m, n), BF16), jax.ShapeDtypeStruct((m, LANES), F32)]
    outs = pl.pallas_call(
        functools.partial(_mm_kernel, nk=nk, seg=seg, w_t=w_t, out_depths=out_depths, n_prev=n_prev,
                          has_res=residual is not None, has_rope=rope is not None, rope_scale=rope_scale,
                          act=act, tn=tn, has_scale=row_ss is not None, norm_next=next_gain is not None),
        grid=(m // tm, n // tn, nk),
        in_specs=in_specs,
        out_specs=out_specs,
        out_shape=out_shapes,
        compiler_params=_params(("parallel", "arbitrary", "arbitrary")),
        name=name,
    )(*args)
    return outs[0] if len(outs) == 1 else outs


def _cast_mm_kernel(*refs, nk, has_res, act):
    it = iter(refs)
    x_ref = next(it)
    w_ref = next(it)
    res_ref = next(it) if has_res else None
    wb_ref = next(it)
    y_ref = next(it)
    k = pl.program_id(1)
    wb = w_ref[...].astype(BF16)
    wb_ref[...] = wb
    part = jnp.dot(x_ref[...].astype(BF16), wb, preferred_element_type=F32)

    @pl.when(k == 0)
    def _():
        y_ref[...] = part

    @pl.when(k > 0)
    def _():
        y_ref[...] += part

    if act is not None or has_res:
        @pl.when(k == nk - 1)
        def _():
            acc = y_ref[...]
            if act == "relu2":
                r = jnp.maximum(acc, 0.0)
                acc = r * r
            if has_res:
                acc = res_ref[...] + acc
            y_ref[...] = acc


def _cast_matmul(x, w_all, layer, n, *, tn, tk, residual=None, act=None, name="cast_matmul"):
    m, kdim = x.shape
    nk = kdim // tk
    in_specs = [pl.BlockSpec((m, tk), lambda j, k: (0, k)),
                pl.BlockSpec((None, tk, tn), lambda j, k: (layer, k, j))]
    args = [x, w_all]
    if residual is not None:
        in_specs.append(pl.BlockSpec((m, tn), lambda j, k: (0, j)))
        args.append(residual)
    wb, y = pl.pallas_call(
        functools.partial(_cast_mm_kernel, nk=nk, has_res=residual is not None, act=act),
        grid=(n // tn, nk),
        in_specs=in_specs,
        out_specs=[pl.BlockSpec((tk, tn), lambda j, k: (k, j)),
                   pl.BlockSpec((m, tn), lambda j, k: (0, j))],
        out_shape=[jax.ShapeDtypeStruct((kdim, n), BF16), jax.ShapeDtypeStruct((m, n), F32)],
        compiler_params=_params(("parallel", "arbitrary")),
        name=name,
    )(*args)
    return y, wb


def _nt_dot(x, wt):
    return lax.dot_general(x, wt, (((1,), (1,)), ((), ())), preferred_element_type=F32)


def _cast_mm_t_kernel(x_ref, w_ref, wb_ref, y_ref):
    wb = w_ref[...].astype(BF16)
    wb_ref[...] = wb
    y_ref[...] = _nt_dot(x_ref[...].astype(BF16), wb)


def _cast_matmul_t(x, wt_all, layer, n, *, tn, name):
    m, kdim = x.shape
    wb, y = pl.pallas_call(
        _cast_mm_t_kernel,
        grid=(n // tn,),
        in_specs=[pl.BlockSpec((m, kdim), lambda j: (0, 0)),
                  pl.BlockSpec((None, tn, kdim), lambda j: (layer, j, 0))],
        out_specs=[pl.BlockSpec((tn, kdim), lambda j: (j, 0)),
                   pl.BlockSpec((m, tn), lambda j: (0, j))],
        out_shape=[jax.ShapeDtypeStruct((n, kdim), BF16), jax.ShapeDtypeStruct((m, n), F32)],
        compiler_params=_params(("parallel",)),
        name=name,
    )(x, wt_all)
    return y, wb


def _cast_tail_kernel(x_ref, a_ref, b_ref, wr_ref, wg_ref, yr_ref, yg_ref):
    j = pl.program_id(0)
    xb = x_ref[...].astype(BF16)
    a = a_ref[...]
    tile = jnp.concatenate([a[SUBLANES:], b_ref[...]], axis=0).astype(BF16)
    wr_ref[...] = tile
    yr_ref[...] = _nt_dot(xb, tile)

    @pl.when(j == 0)
    def _():
        pad = jnp.zeros((LANES - SUBLANES, a.shape[1]), F32)
        g = jnp.concatenate([a[:SUBLANES], pad], axis=0).astype(BF16)
        wg_ref[...] = g
        yg_ref[...] = _nt_dot(xb, g)


def _cast_tail(x, wt_all, layer, *, tn, name):
    assert 2 * H_M == SUBLANES
    m, kdim = x.shape
    n = 2 * D_RG
    blk0 = D_MAIN // tn
    return pl.pallas_call(
        _cast_tail_kernel,
        grid=(n // tn,),
        in_specs=[pl.BlockSpec((m, kdim), lambda j: (0, 0)),
                  pl.BlockSpec((None, tn, kdim), lambda j: (layer, blk0 + j, 0)),
                  pl.BlockSpec((None, SUBLANES, kdim), lambda j: (layer, (blk0 + j + 1) * (tn // SUBLANES), 0))],
        out_specs=[pl.BlockSpec((tn, kdim), lambda j: (j, 0)),
                   pl.BlockSpec((LANES, kdim), lambda j: (0, 0)),
                   pl.BlockSpec((m, tn), lambda j: (0, j)),
                   pl.BlockSpec((m, LANES), lambda j: (0, 0))],
        out_shape=[jax.ShapeDtypeStruct((n, kdim), BF16), jax.ShapeDtypeStruct((LANES, kdim), BF16),
                   jax.ShapeDtypeStruct((m, n), F32), jax.ShapeDtypeStruct((m, LANES), F32)],
        compiler_params=_params(("arbitrary",)),
        name=name,
    )(x, wt_all, wt_all)


def _rope_rows_kernel(x_ref, cos_ref, sin_ref, o_ref, *, n_q):
    cos = cos_ref[...]
    sin = sin_ref[...]
    for g in range(x_ref.shape[1] // DH_DA):
        sl = slice(g * DH_DA, (g + 1) * DH_DA)
        y = _rotate(x_ref[:, sl], cos, sin)
        o_ref[:, sl] = y * ATTN_QSCALE if g < n_q else y


def _rope_rows(x, cos, sin):
    m, n = x.shape
    full = lambda c: pl.BlockSpec((m, c), lambda i: (0, 0))
    return pl.pallas_call(
        functools.partial(_rope_rows_kernel, n_q=D_ATTN // DH_DA),
        grid=(1,),
        in_specs=[full(n), full(DH_DA), full(DH_DA)],
        out_specs=full(n),
        out_shape=jax.ShapeDtypeStruct((m, n), F32),
        compiler_params=_params(("arbitrary",)),
        name="rope_sample",
    )(x, cos, sin)


def _attn_prompt_kernel(lq1, lk1, lq2, lk2, g_ref, q_ref, k_ref, v_ref, o_ref, *, t, nq, lam_init):
    qi = pl.program_id(2)
    q = q_ref[...]
    lam = _lambda(lq1, lk1, lq2, lk2, lam_init)
    tri = (lax.broadcasted_iota(jnp.int32, (t, t), 1) <= lax.broadcasted_iota(jnp.int32, (t, t), 0))

    def block(n_below):
        spans = ([(0, n_below * t, False)] if n_below else []) + [(n_below * t, (n_below + 1) * t, True)]
        probs, inv = [], []
        for c in range(2):
            sl = slice(c * DH_DA, (c + 1) * DH_DA)
            ss = []
            for lo, hi, diagonal in spans:
                s = lax.dot_general(q[:, sl], k_ref[lo:hi, sl], (((1,), (1,)), ((), ())),
                                    preferred_element_type=F32)
                ss.append(jnp.where(tri, s, NEG) if diagonal else s)
            m = functools.reduce(jnp.maximum, [jnp.max(s, axis=1, keepdims=True) for s in ss])
            ps = [jnp.exp2(s - m) for s in ss]
            probs.append(ps)
            inv.append(1.0 / functools.reduce(jnp.add, [jnp.sum(p, axis=1, keepdims=True) for p in ps]))
        r1 = inv[0]
        r2 = lam * inv[1]
        out = None
        for i, (lo, hi, _) in enumerate(spans):
            a = (probs[0][i] * r1 - probs[1][i] * r2).astype(BF16)
            d = jnp.dot(a, v_ref[lo:hi, :], preferred_element_type=F32)
            out = d if out is None else out + d
        ms = jnp.mean(out * out, axis=-1, keepdims=True)
        y = out * lax.rsqrt(ms + SUBLN_EPS) * g_ref[...]
        o_ref[...] = (y * (1.0 - lam_init)).astype(o_ref.dtype)

    for n_below in range(nq):
        pl.when(qi == n_below)(functools.partial(block, n_below))


def _attn_prompt(q, k, v, lam_rows, subln_g, lam_init, batch, seq, t=512):
    nq = seq // t
    dh2 = 2 * DH_DA
    row_spec = pl.BlockSpec((1, DH_DA), lambda b, h, i: (0, 0))
    return pl.pallas_call(
        functools.partial(_attn_prompt_kernel, t=t, nq=nq, lam_init=lam_init),
        grid=(batch, H_DA, nq),
        in_specs=[row_spec, row_spec, row_spec, row_spec,
                  pl.BlockSpec((1, dh2), lambda b, h, i: (0, 0)),
                  pl.BlockSpec((t, dh2), lambda b, h, i: (b * nq + i, h)),
                  pl.BlockSpec((seq, dh2), lambda b, h, i: (b, h)),
                  pl.BlockSpec((seq, dh2), lambda b, h, i: (b, h))],
        out_specs=pl.BlockSpec((t, dh2), lambda b, h, i: (b * nq + i, h)),
        out_shape=jax.ShapeDtypeStruct((batch * seq, D_ATTN), BF16),
        compiler_params=_params(("parallel", "parallel", "arbitrary")),
        name="attn_prompt",
    )(*lam_rows, subln_g.reshape(1, dh2), q, k, v)


def _attn_sample_kernel(*refs, lam_init, n_steps, group):
    pt_ref, lq1, lk1, lq2, lk2, g_ref, q_ref, kn_ref, vn_ref = refs[:9]
    k_refs = refs[9:9 + group]
    v_refs = refs[9 + group:9 + 2 * group]
    o_ref, m_sc, l_sc, acc_sc = refs[9 + 2 * group:]
    del pt_ref
    step = pl.program_id(1)
    rows = PAGE_SIZE * H_DA

    @pl.when(step == 0)
    def _():
        m_sc[...] = jnp.full(m_sc.shape, NEG, F32)
        l_sc[...] = jnp.zeros(l_sc.shape, F32)
        acc_sc[...] = jnp.zeros(acc_sc.shape, F32)

    q = q_ref[...]
    zero = jnp.zeros((H_DA, DH_DA), F32)
    qbd = jnp.concatenate([jnp.concatenate([q[:, :DH_DA], zero], axis=1),
                           jnp.concatenate([zero, q[:, DH_DA:]], axis=1)], axis=0).astype(BF16)
    own_head = (lax.broadcasted_iota(jnp.int32, (2 * H_DA, rows), 0) % H_DA
                == lax.broadcasted_iota(jnp.int32, (2 * H_DA, rows), 1) % H_DA)
    scores = []
    for g in range(group):
        k2 = k_refs[g][...].reshape(rows, 2 * DH_DA).astype(BF16)
        s = lax.dot_general(qbd, k2, (((1,), (1,)), ((), ())), preferred_element_type=F32)
        scores.append(jnp.where(own_head, s, NEG))
    m_prev = m_sc[...]
    m_new = m_prev
    for s in scores:
        m_new = jnp.maximum(m_new, jnp.max(s, axis=1, keepdims=True))
    alpha = jnp.exp2(m_prev - m_new)
    l_new = alpha * l_sc[...]
    acc = alpha * acc_sc[...]
    for g in range(group):
        p = jnp.exp2(scores[g] - m_new)
        l_new = l_new + jnp.sum(p, axis=1, keepdims=True)
        v2 = v_refs[g][...].reshape(rows, 2 * DH_DA).astype(BF16)
        acc = acc + jnp.dot(p.astype(BF16), v2, preferred_element_type=F32)
    m_sc[...] = m_new
    l_sc[...] = l_new
    acc_sc[...] = acc

    @pl.when(step == n_steps - 1)
    def _():
        kn = kn_ref[...]
        vn = vn_ref[...]
        pn = kn * q
        sn = jnp.concatenate([jnp.sum(pn[:, :DH_DA], axis=-1, keepdims=True),
                              jnp.sum(pn[:, DH_DA:], axis=-1, keepdims=True)], axis=0)
        m_fin = jnp.maximum(m_new, sn)
        pe = jnp.exp2(sn - m_fin)
        a_fin = jnp.exp2(m_new - m_fin)
        l_fin = a_fin * l_new + pe
        outs = (a_fin * acc + pe * jnp.concatenate([vn, vn], axis=0)) / l_fin
        lam = _lambda(lq1, lk1, lq2, lk2, lam_init)
        out = outs[0:H_DA] - lam * outs[H_DA:]
        ms = jnp.mean(out * out, axis=-1, keepdims=True)
        y = out * lax.rsqrt(ms + SUBLN_EPS) * g_ref[...]
        o_ref[...] = (y * (1.0 - lam_init)).astype(o_ref.dtype)


def _attn_sample(page_table, q, k_new, v_new, cache_k, cache_v, layer, lam_rows, subln_g, lam_init, group=8):
    bd, n_pages = page_table.shape
    n_steps = n_pages // group
    dh2 = 2 * DH_DA
    row_spec = pl.BlockSpec((1, DH_DA), lambda b, p, pt: (0, 0))
    tok_spec = pl.BlockSpec((None, H_DA, dh2), lambda b, p, pt: (b, 0, 0))

    def page_spec(g):
        return pl.BlockSpec((None, None, PAGE_SIZE, H_DA, dh2),
                            lambda b, p, pt: (layer, pt[b, p * group + g], 0, 0, 0))

    pages = [page_spec(g) for g in range(group)]
    grid_spec = pltpu.PrefetchScalarGridSpec(
        num_scalar_prefetch=1,
        grid=(bd, n_steps),
        in_specs=[row_spec, row_spec, row_spec, row_spec,
                  pl.BlockSpec((1, dh2), lambda b, p, pt: (0, 0)),
                  tok_spec, tok_spec, tok_spec] + pages + pages,
        out_specs=tok_spec,
        scratch_shapes=[pltpu.VMEM((2 * H_DA, 1), F32), pltpu.VMEM((2 * H_DA, 1), F32),
                        pltpu.VMEM((2 * H_DA, dh2), F32)],
    )
    return pl.pallas_call(
        functools.partial(_attn_sample_kernel, lam_init=lam_init, n_steps=n_steps, group=group),
        grid_spec=grid_spec,
        out_shape=jax.ShapeDtypeStruct((bd, H_DA, dh2), F32),
        compiler_params=_params(("parallel", "arbitrary")),
        name="attn_sample",
    )(page_table, *lam_rows, subln_g.reshape(1, dh2), q, k_new, v_new,
      *([cache_k] * group), *([cache_v] * group))


def _mlstm_out(h, g, om):
    mu = jnp.mean(h, axis=-1, keepdims=True)
    hc = h - mu
    var = jnp.mean(hc * hc, axis=-1, keepdims=True)
    return hc * lax.rsqrt(var + NORM_EPS) * g * _sigmoid(om)


def _mlstm_prompt_kernel(big_ref, bfg_ref, q_ref, k_ref, v_ref, om_ref, gc_ref, gr_ref, ng_ref,
                         hm_ref, c_out, n_out, m_out, c_sc, n_sc, m_sc, *, n_chunks):
    ci = pl.program_id(1)
    L = MLSTM_CHUNK

    @pl.when(ci == 0)
    def _():
        c_sc[...] = jnp.zeros(c_sc.shape, F32)
        n_sc[...] = jnp.zeros(n_sc.shape, F32)
        m_sc[...] = jnp.zeros(m_sc.shape, F32)

    ri = lax.broadcasted_iota(jnp.int32, (L, L), 0)
    cj = lax.broadcasted_iota(jnp.int32, (L, L), 1)
    lower = ri >= cj
    gc = gc_ref[...]
    gr = gr_ref[...]
    for h in range(H_M):
        hs = slice(h * DH_M, (h + 1) * DH_M)
        ig_c = gc[:, h:h + 1] + big_ref[h]
        lf_c = _log_sigmoid(gc[:, H_M + h:H_M + h + 1] + bfg_ref[h])
        ig_r = gr[h:h + 1, :] + big_ref[h]
        lf_r = _log_sigmoid(gr[H_M + h:H_M + h + 1, :] + bfg_ref[h])
        b_col = jnp.sum(jnp.where(lower, lf_r, 0.0), axis=1, keepdims=True)
        b_row = jnp.sum(jnp.where(cj >= ri, lf_c, 0.0), axis=0, keepdims=True)
        dm = jnp.where(lower, b_col - b_row + ig_r, -jnp.inf)
        m_prev = m_sc[h:h + 1, 0:1]
        inter = b_col + m_prev
        m_t = jnp.maximum(inter, jnp.max(dm, axis=1, keepdims=True))
        w_intra = jnp.exp(dm - m_t)
        w_inter = jnp.exp(inter - m_t)
        q = q_ref[:, hs]
        ks = k_ref[:, hs] * (DH_M ** -0.5)
        v = v_ref[:, hs]
        c_prev = c_sc[h]
        n_prev = n_sc[h:h + 1, :]
        sc = w_intra * lax.dot_general(q, ks, (((1,), (1,)), ((), ())), preferred_element_type=F32)
        num = (jnp.dot(sc.astype(BF16), v, preferred_element_type=F32)
               + w_inter * lax.dot_general(q, c_prev.astype(BF16), (((1,), (1,)), ((), ())),
                                           preferred_element_type=F32))
        den = (jnp.sum(sc, axis=1, keepdims=True)
               + w_inter * jnp.sum(q.astype(F32) * n_prev, axis=1, keepdims=True))
        hh = num / jnp.maximum(jnp.abs(den), jnp.exp(-m_t))
        m_new = m_t[L - 1:L, :]
        decay = w_inter[L - 1:L, :]
        wk = jnp.exp(b_col[L - 1:L, :] - b_col + ig_c - m_new)
        wv = (wk * v.astype(F32)).astype(BF16)
        c_sc[h] = decay * c_prev + lax.dot_general(wv, ks, (((0,), (0,)), ((), ())),
                                                   preferred_element_type=F32)
        n_sc[h:h + 1, :] = decay * n_prev + jnp.sum(wk * ks.astype(F32), axis=0, keepdims=True)
        m_sc[h:h + 1, :] = jnp.broadcast_to(m_new, (1, LANES))
        hm_ref[:, hs] = _mlstm_out(hh, ng_ref[:, hs], om_ref[:, hs].astype(F32)).astype(hm_ref.dtype)

    @pl.when(ci == n_chunks - 1)
    def _():
        c_out[...] = c_sc[...]
        n_out[...] = n_sc[0:H_M, :]
        m_out[...] = m_sc[0:H_M, :]


def _mlstm_prompt(um, gates, gates_t, b_ig, b_fg, norm_g, batch, seq):
    L = MLSTM_CHUNK
    nc = seq // L
    smem = pl.BlockSpec(memory_space=pltpu.SMEM)

    def col(cb):
        return pl.BlockSpec((L, D_MLSTM), lambda b, c: (b * nc + c, cb))

    return pl.pallas_call(
        functools.partial(_mlstm_prompt_kernel, n_chunks=nc),
        grid=(batch, nc),
        in_specs=[smem, smem, col(0), col(1), col(2), col(3),
                  pl.BlockSpec((L, LANES), lambda b, c: (b * nc + c, 0)),
                  pl.BlockSpec((None, SUBLANES, L), lambda b, c: (b, 0, c)),
                  pl.BlockSpec((1, D_MLSTM), lambda b, c: (0, 0))],
        out_specs=[pl.BlockSpec((L, D_MLSTM), lambda b, c: (b * nc + c, 0)),
                   pl.BlockSpec((None, H_M, DH_M, DH_M), lambda b, c: (b, 0, 0, 0)),
                   pl.BlockSpec((None, H_M, DH_M), lambda b, c: (b, 0, 0)),
                   pl.BlockSpec((None, H_M, LANES), lambda b, c: (b, 0, 0))],
        out_shape=[jax.ShapeDtypeStruct((batch * seq, D_MLSTM), BF16),
                   jax.ShapeDtypeStruct((batch, H_M, DH_M, DH_M), F32),
                   jax.ShapeDtypeStruct((batch, H_M, DH_M), F32),
                   jax.ShapeDtypeStruct((batch, H_M, LANES), F32)],
        scratch_shapes=[pltpu.VMEM((H_M, DH_M, DH_M), F32), pltpu.VMEM((SUBLANES, DH_M), F32),
                        pltpu.VMEM((SUBLANES, LANES), F32)],
        compiler_params=_params(("parallel", "arbitrary")),
        name="mlstm_prompt",
    )(b_ig, b_fg, um, um, um, um, gates, gates_t, norm_g.reshape(1, D_MLSTM))


def _mlstm_sample_kernel(big_ref, bfg_ref, u_ref, vcol_ref, g_ref, ng_ref, c_ref, n_ref, m_ref,
                         hm_ref, c_out, n_out, m_out):
    u = u_ref[...]
    g = g_ref[...]
    for h in range(H_M):
        def part(i):
            return u[:, i * D_MLSTM + h * DH_M:i * D_MLSTM + (h + 1) * DH_M]
        q, k, v, om = part(0), part(1), part(2), part(3)
        ks = k * (DH_M ** -0.5)
        ig = g[:, h:h + 1] + big_ref[h]
        lf = _log_sigmoid(g[:, H_M + h:H_M + h + 1] + bfg_ref[h])
        m_prev = m_ref[:, h:h + 1]
        inter = lf + m_prev
        m_t = jnp.maximum(inter, ig)
        w_intra = jnp.exp(ig - m_t)
        w_inter = jnp.exp(inter - m_t)
        c_prev = c_ref[h]
        n_prev = n_ref[h:h + 1, :]
        q8 = jnp.broadcast_to(q, (2 * SUBLANES, DH_M)).astype(BF16)
        cq = lax.dot_general(q8, c_prev.astype(BF16), (((1,), (1,)), ((), ())),
                             preferred_element_type=F32)[0:1, :]
        sc = w_intra * jnp.sum(q * ks, axis=-1, keepdims=True)
        num = sc * v + w_inter * cq
        den = sc + w_inter * jnp.sum(n_prev * q, axis=-1, keepdims=True)
        hh = num / jnp.maximum(jnp.abs(den), jnp.exp(-m_t))
        c_out[h] = w_inter * c_prev + (w_intra * vcol_ref[h]) * ks
        n_out[h:h + 1, :] = w_inter * n_prev + w_intra * ks
        m_out[h:h + 1, :] = jnp.broadcast_to(m_t, (1, LANES))
        hs = slice(h * DH_M, (h + 1) * DH_M)
        hm_ref[:, hs] = _mlstm_out(hh, ng_ref[:, hs], om)


def _mlstm_sample(um, gates, b_ig, b_fg, norm_g, c_state, n_state, m_state):
    bd = um.shape[0]
    smem = pl.BlockSpec(memory_space=pltpu.SMEM)
    u4 = um.reshape(bd, 1, 4 * D_MLSTM)
    vcol = um[:, 2 * D_MLSTM:3 * D_MLSTM].reshape(bd, H_M, DH_M, 1)
    return pl.pallas_call(
        _mlstm_sample_kernel,
        grid=(bd,),
        in_specs=[smem, smem,
                  pl.BlockSpec((None, 1, 4 * D_MLSTM), lambda b: (b, 0, 0)),
                  pl.BlockSpec((None, H_M, DH_M, 1), lambda b: (b, 0, 0, 0)),
                  pl.BlockSpec((None, 1, LANES), lambda b: (b, 0, 0)),
                  pl.BlockSpec((1, D_MLSTM), lambda b: (0, 0)),
                  pl.BlockSpec((None, H_M, DH_M, DH_M), lambda b: (b, 0, 0, 0)),
                  pl.BlockSpec((None, H_M, DH_M), lambda b: (b, 0, 0)),
                  pl.BlockSpec((None, 1, H_M), lambda b: (b, 0, 0))],
        out_specs=[pl.BlockSpec((None, 1, D_MLSTM), lambda b: (b, 0, 0)),
                   pl.BlockSpec((None, H_M, DH_M, DH_M), lambda b: (b, 0, 0, 0)),
                   pl.BlockSpec((None, H_M, DH_M), lambda b: (b, 0, 0)),
                   pl.BlockSpec((None, H_M, LANES), lambda b: (b, 0, 0))],
        out_shape=[jax.ShapeDtypeStruct((bd, 1, D_MLSTM), F32),
                   jax.ShapeDtypeStruct((bd, H_M, DH_M, DH_M), F32),
                   jax.ShapeDtypeStruct((bd, H_M, DH_M), F32),
                   jax.ShapeDtypeStruct((bd, H_M, LANES), F32)],
        compiler_params=_params(("parallel",)),
        name="mlstm_sample",
    )(b_ig, b_fg, u4, vcol, gates.reshape(bd, 1, LANES), norm_g.reshape(1, D_MLSTM),
      c_state, n_state, m_state.reshape(bd, 1, H_M))


def _rglru_coeffs(xc, wra_ref, bra_ref, wrx_ref, brx_ref, lam_ref):
    rs, is_ = [], []
    for n in range(RG_BLOCKS):
        xb = xc[:, n * RG_BW:(n + 1) * RG_BW].astype(BF16)
        rs.append(jnp.dot(xb, wra_ref[n], preferred_element_type=F32))
        is_.append(jnp.dot(xb, wrx_ref[n], preferred_element_type=F32))
    r = _sigmoid(jnp.concatenate(rs, axis=-1) + bra_ref[...])
    i = _sigmoid(jnp.concatenate(is_, axis=-1) + brx_ref[...])
    log_a = -RG_C * r * _softplus(-lam_ref[...])
    a = jnp.exp(log_a)
    mult = jnp.sqrt(1.0 - a * a)
    return a, mult * i * xc


def _rglru_prompt_kernel(x_ref, gate_ref, cw_ref, cb_ref, wra_ref, bra_ref, wrx_ref, brx_ref, lam_ref,
                         y_ref, conv_out, h_out, xbuf, a_sc, b_sc, h_sc, *, tt, n_tiles):
    ti = pl.program_id(1)
    pad = SUBLANES

    @pl.when(ti == 0)
    def _():
        xbuf[0:pad, :] = jnp.zeros((pad, D_RG), F32)
        h_sc[...] = jnp.zeros(h_sc.shape, F32)

    xbuf[pad:pad + tt, :] = x_ref[...].astype(F32)
    xc = cb_ref[...] + sum(xbuf[pad - (CONV_W - 1) + j:pad - (CONV_W - 1) + j + tt, :] * cw_ref[j:j + 1, :]
                           for j in range(CONV_W))
    a, b = _rglru_coeffs(xc, wra_ref, bra_ref, wrx_ref, brx_ref, lam_ref)
    a_sc[...] = a
    b_sc[...] = b

    def step(t, h):
        h = a_sc[pl.ds(t, 1), :] * h + b_sc[pl.ds(t, 1), :]
        b_sc[pl.ds(t, 1), :] = h
        return h

    h_last = lax.fori_loop(0, tt, step, h_sc[...], unroll=8)
    h_sc[...] = h_last
    y_ref[...] = (b_sc[...] * _gelu_tanh(gate_ref[...].astype(F32))).astype(y_ref.dtype)
    tail = xbuf[tt:tt + pad, :]
    xbuf[0:pad, :] = tail

    @pl.when(ti == n_tiles - 1)
    def _():
        conv_out[...] = tail[pad - (CONV_W - 1):, :]
        h_out[...] = h_last


def _rglru_weights(conv_w, conv_b, w_ra, b_ra, w_rx, b_rx, lam):
    row = lambda z: z.reshape(1, D_RG).astype(F32)
    return (conv_w.astype(F32), row(conv_b), w_ra.astype(BF16), row(b_ra), w_rx.astype(BF16), row(b_rx), row(lam))


def _rglru_weight_specs():
    zeros2 = (lambda *a: (0, 0))
    zeros3 = (lambda *a: (0, 0, 0))
    return [pl.BlockSpec((CONV_W, D_RG), zeros2), pl.BlockSpec((1, D_RG), zeros2),
            pl.BlockSpec((RG_BLOCKS, RG_BW, RG_BW), zeros3), pl.BlockSpec((1, D_RG), zeros2),
            pl.BlockSpec((RG_BLOCKS, RG_BW, RG_BW), zeros3), pl.BlockSpec((1, D_RG), zeros2),
            pl.BlockSpec((1, D_RG), zeros2)]


def _rglru_prompt(urg, weights, batch, seq, tt=256):
    nt = seq // tt
    return pl.pallas_call(
        functools.partial(_rglru_prompt_kernel, tt=tt, n_tiles=nt),
        grid=(batch, nt),
        in_specs=[pl.BlockSpec((tt, D_RG), lambda b, t: (b * nt + t, 0)),
                  pl.BlockSpec((tt, D_RG), lambda b, t: (b * nt + t, 1))] + _rglru_weight_specs(),
        out_specs=[pl.BlockSpec((tt, D_RG), lambda b, t: (b * nt + t, 0)),
                   pl.BlockSpec((None, CONV_W - 1, D_RG), lambda b, t: (b, 0, 0)),
                   pl.BlockSpec((None, 1, D_RG), lambda b, t: (b, 0, 0))],
        out_shape=[jax.ShapeDtypeStruct((batch * seq, D_RG), BF16),
                   jax.ShapeDtypeStruct((batch, CONV_W - 1, D_RG), F32),
                   jax.ShapeDtypeStruct((batch, 1, D_RG), F32)],
        scratch_shapes=[pltpu.VMEM((tt + SUBLANES, D_RG), F32), pltpu.VMEM((tt, D_RG), F32),
                        pltpu.VMEM((tt, D_RG), F32), pltpu.VMEM((1, D_RG), F32)],
        compiler_params=_params(("parallel", "arbitrary")),
        name="rglru_prompt",
    )(urg, urg, *weights)


def _rglru_sample_kernel(x_ref, gate_ref, p0_ref, p1_ref, p2_ref, h0_ref, cw_ref, cb_ref, wra_ref, bra_ref,
                         wrx_ref, brx_ref, lam_ref, y_ref, h_out):
    x = x_ref[...]
    xc = cb_ref[...] + (p0_ref[...] * cw_ref[0:1, :] + p1_ref[...] * cw_ref[1:2, :]
                        + p2_ref[...] * cw_ref[2:3, :] + x * cw_ref[3:4, :])
    a, b = _rglru_coeffs(xc, wra_ref, bra_ref, wrx_ref, brx_ref, lam_ref)
    h = a * h0_ref[...] + b
    h_out[...] = h
    y_ref[...] = h * _gelu_tanh(gate_ref[...])


def _rglru_sample(x, gate, conv_prev, h0, weights):
    bd = x.shape[0]
    full = pl.BlockSpec((bd, D_RG), lambda i: (0, 0))
    return pl.pallas_call(
        _rglru_sample_kernel,
        grid=(1,),
        in_specs=[full] * 6 + _rglru_weight_specs(),
        out_specs=[full, full],
        out_shape=[jax.ShapeDtypeStruct((bd, D_RG), F32)] * 2,
        compiler_params=_params(("arbitrary",)),
        name="rglru_sample",
    )(x, gate, conv_prev[:, 0], conv_prev[:, 1], conv_prev[:, 2], h0, *weights)


def _rope_tables(pos):
    inv = 1.0 / (ROPE_THETA ** (jnp.arange(0, DH_DA, 2, dtype=F32) / DH_DA))
    ang = pos.astype(F32)[:, None] * inv[None, :]
    ang = jnp.concatenate([ang, ang], axis=-1)
    sign = jnp.concatenate([-jnp.ones((DH_DA // 2,), F32), jnp.ones((DH_DA // 2,), F32)])
    return jnp.cos(ang), jnp.sin(ang) * sign[None, :]


def kernel(x_prompt, x_sample, cache_k, cache_v, page_table, state_mlstm_c, state_mlstm_n, state_mlstm_m, state_conv, state_rglru_h, norm_mix_g, w_in, w_out, lam_q1, lam_k1, lam_q2, lam_k2, attn_subln_g, b_ig, b_fg, mlstm_norm_g, conv_w, conv_b, w_ra, b_ra, w_rx, b_rx, rg_lambda, norm_mlp_g, w_up, w_down, final_norm_g):
    bp, sp, _ = x_prompt.shape
    bd, td, _ = x_sample.shape
    assert td == 1
    past = page_table.shape[1] * PAGE_SIZE
    mp = bp * sp
    dh2 = 2 * DH_DA
    cos_p, sin_p = _rope_tables(jnp.arange(sp))
    cos_s, sin_s = _rope_tables(jnp.full((bd,), past))

    TM = 1024
    TNC = 512

    xp = x_prompt.reshape(mp, D_MODEL)
    xs = x_sample.reshape(bd, D_MODEL)
    w_in_t = jnp.swapaxes(w_in, 1, 2)
    k_all = v_all = None
    outs = {n: [] for n in ("ks", "vs", "cp", "np", "mp", "cs", "ns", "ms", "cvp", "cvs", "hp", "hs")}
    for l in range(DEPTH):
        lam_init = 0.8 - 0.6 * math.exp(-0.3 * l)
        lam_rows = [z[l].reshape(1, DH_DA).astype(F32) for z in (lam_q1, lam_k1, lam_q2, lam_k2)]
        rg_w = _rglru_weights(conv_w[l], conv_b[l], w_ra[l], b_ra[l], w_rx[l], b_rx[l], rg_lambda[l])

        hs_n = _rmsnorm(xs, norm_mix_g[l], NORM_EPS, F32, tm=bd)
        u_s, w_in_b = _cast_matmul_t(hs_n, w_in_t, l, D_MAIN, tn=TNC, name="s_proj_in")
        wr, wg, urg_s, gates_s = _cast_tail(hs_n, w_in_t, l, tn=TNC, name="s_proj_tail")
        qk_s = _rope_rows(u_s[:, :2 * D_ATTN], cos_s, sin_s)
        q_s, k_s = qk_s[:, :D_ATTN], qk_s[:, D_ATTN:]
        v_s = u_s[:, 2 * D_ATTN:3 * D_ATTN]
        um_s = u_s[:, 3 * D_ATTN:]

        if l == 0:
            hn, ss = _rmsnorm(xp, norm_mix_g[l], NORM_EPS, BF16, tm=512), None
        mm = functools.partial(_matmul, hn, tm=TM, tk=D_MODEL, w_t=True, row_ss=ss)
        prev_k = None if k_all is None else [k_all, None]
        prev_v = None if v_all is None else [v_all, None]
        q = mm(w_in_b, n=D_ATTN, w_col0=0, tn=1024, out_dtypes=(BF16,), rope=(cos_p, sin_p),
               rope_scale=ATTN_QSCALE, name="proj_q")
        tn_kv = 1024 if l == 0 else 512
        k_all, k16 = mm(w_in_b, n=D_ATTN, w_col0=D_ATTN, tn=tn_kv, out_dtypes=(F32, BF16), rope=(cos_p, sin_p),
                        stacked=[True, False], prev=prev_k, name="proj_k")
        v_all, v16 = mm(w_in_b, n=D_ATTN, w_col0=2 * D_ATTN, tn=tn_kv, out_dtypes=(F32, BF16),
                        stacked=[True, False], prev=prev_v, name="proj_v")
        um = mm(w_in_b, n=4 * D_MLSTM, w_col0=3 * D_ATTN, tn=1024, out_dtypes=(BF16,), name="proj_m")
        urg = mm(wr, tn=1024, out_dtypes=(BF16,), name="proj_r")
        gates = mm(wg, tn=LANES, out_dtypes=(F32,), name="proj_g")
        gates_t = gates[:, :SUBLANES].reshape(bp, sp, SUBLANES).transpose(0, 2, 1)
        att = _attn_prompt(q, k16, v16, lam_rows, attn_subln_g[l], lam_init, bp, sp)
        hm, c_p, n_p, m_p = _mlstm_prompt(um, gates, gates_t, b_ig[l], b_fg[l], mlstm_norm_g[l], bp, sp)
        hr, cv_p, h_p = _rglru_prompt(urg, rg_w, bp, sp)

        att_s = _attn_sample(page_table, q_s.reshape(bd, H_DA, dh2), k_s.reshape(bd, H_DA, dh2),
                             v_s.reshape(bd, H_DA, dh2), cache_k, cache_v, l, lam_rows,
                             attn_subln_g[l], lam_init)
        hm_s, c_s, n_s, m_s = _mlstm_sample(um_s, gates_s, b_ig[l], b_fg[l], mlstm_norm_g[l],
                                            state_mlstm_c[l], state_mlstm_n[l], state_mlstm_m[l])
        x_rg = urg_s[:, :D_RG]
        hr_s, h_s = _rglru_sample(x_rg, urg_s[:, D_RG:], state_conv[l], state_rglru_h[l], rg_w)
        cat_s = jnp.concatenate([att_s.reshape(bd, D_ATTN), hm_s.reshape(bd, D_MLSTM), hr_s], axis=1)
        x1s, wo = _cast_matmul(cat_s, w_out, l, D_MODEL, tn=TNC, tk=D_MODEL, residual=xs, name="s_proj_out")
        hs_n2 = _rmsnorm(x1s, norm_mlp_g[l], NORM_EPS, F32, tm=bd)
        act_s, wu = _cast_matmul(hs_n2, w_up, l, D_FF, tn=TNC, tk=D_MODEL, act="relu2", name="s_mlp_up")
        xs, wd = _cast_matmul(act_s, w_down, l, D_MODEL, tn=TNC, tk=D_MODEL, residual=x1s, name="s_mlp_down")

        x1, xg1, ss1 = _matmul([att, hm, hr], wo, tm=TM, tn=512, tk=D_MODEL, out_dtypes=(F32,), residual=xp,
                               next_gain=norm_mlp_g[l], name="proj_out")
        act = _matmul(xg1, wu, tm=TM, tn=1024, tk=D_MODEL, out_dtypes=(BF16,), act="relu2", row_ss=ss1,
                      name="mlp_up")
        down = functools.partial(_matmul, act, wd, tm=TM, tn=1024, tk=D_MODEL, out_dtypes=(F32,), residual=x1,
                                 name="mlp_down")
        if l + 1 < DEPTH:
            xp, hn, ss = down(next_gain=norm_mix_g[l + 1])
        else:
            xp = down()

        outs["ks"].append(k_s.reshape(bd, td, H_DA, dh2))
        outs["vs"].append(v_s.reshape(bd, td, H_DA, dh2))
        outs["cp"].append(c_p)
        outs["np"].append(n_p)
        outs["mp"].append(m_p[:, :, 0])
        outs["cs"].append(c_s)
        outs["ns"].append(n_s)
        outs["ms"].append(m_s[:, :, 0])
        outs["cvp"].append(cv_p)
        outs["cvs"].append(jnp.concatenate([state_conv[l][:, 1:], x_rg[:, None, :]], axis=1))
        outs["hp"].append(h_p.reshape(bp, D_RG))
        outs["hs"].append(h_s)

    y_prompt = _rmsnorm(xp, final_norm_g, NORM_EPS, F32, tm=512).reshape(bp, sp, D_MODEL)
    y_sample = _rmsnorm(xs, final_norm_g, NORM_EPS, F32, tm=bd).reshape(bd, td, D_MODEL)
    st = lambda n: jnp.stack(outs[n])
    return (y_prompt, y_sample, k_all.reshape(DEPTH, bp, sp, H_DA, dh2), v_all.reshape(DEPTH, bp, sp, H_DA, dh2),
            st("ks"), st("vs"), st("cp"), st("np"), st("mp"),
            st("cs"), st("ns"), st("ms"), st("cvp"), st("cvs"), st("hp"), st("hs"))
```

```python
import functools
import math

import jax
import jax.numpy as jnp
from jax import lax
from jax.experimental import pallas as pl
from jax.experimental.pallas import tpu as pltpu

D_MODEL = 4096
DEPTH = 2
PAGE_SIZE = 128
D_ATTN = D_MODEL // 2
D_MLSTM = D_MODEL // 4
D_RG = D_MODEL - D_ATTN - D_MLSTM
DH_DA = 128
H_DA = D_ATTN // (2 * DH_DA)
H_M = 4
DH_M = D_MLSTM // H_M
RG_BLOCKS = 8
RG_BW = D_RG // RG_BLOCKS
CONV_W = 4
RG_C = 8.0
D_FF = 4 * D_MODEL
ROPE_THETA = 10000.0
MLSTM_CHUNK = 128
NORM_EPS = 1e-6
SUBLN_EPS = 1e-5
NEG = -1e30
D_MAIN = 3 * D_ATTN + 4 * D_MLSTM
ATTN_QSCALE = DH_DA ** -0.5 * math.log2(math.e)

MXU_COLS = 256
LANES = 128
SUBLANES = 8
VMEM_LIMIT = 56 * 1024 * 1024

F32 = jnp.float32
BF16 = jnp.bfloat16


def _params(sem):
    return pltpu.CompilerParams(dimension_semantics=sem, vmem_limit_bytes=VMEM_LIMIT)


def _sigmoid(x):
    return 1.0 / (1.0 + jnp.exp(-x))


def _softplus(x):
    return jnp.maximum(x, 0.0) + jnp.log1p(jnp.exp(-jnp.abs(x)))


def _log_sigmoid(x):
    return -_softplus(-x)


def _gelu_tanh(x):
    c = math.sqrt(2.0 / math.pi)
    return 0.5 * x * (1.0 + jnp.tanh(c * (x + 0.044715 * (x * x * x))))


def _lambda(lq1, lk1, lq2, lk2, lam_init):
    a = jnp.exp(jnp.sum(lq1[...] * lk1[...], axis=-1, keepdims=True))
    b = jnp.exp(jnp.sum(lq2[...] * lk2[...], axis=-1, keepdims=True))
    return a - b + lam_init


def _rotate(x, cos, sin_signed):
    return x * cos + pltpu.roll(x, DH_DA // 2, axis=1) * sin_signed


def _rmsnorm_kernel(x_ref, g_ref, o_ref, *, eps):
    x = x_ref[...].astype(F32)
    ms = jnp.mean(x * x, axis=-1, keepdims=True)
    o_ref[...] = (x * lax.rsqrt(ms + eps) * g_ref[...]).astype(o_ref.dtype)


def _rmsnorm(x, g, eps, out_dtype, tm):
    m, d = x.shape
    return pl.pallas_call(
        functools.partial(_rmsnorm_kernel, eps=eps),
        grid=(m // tm,),
        in_specs=[pl.BlockSpec((tm, d), lambda i: (i, 0)),
                  pl.BlockSpec((1, d), lambda i: (0, 0))],
        out_specs=pl.BlockSpec((tm, d), lambda i: (i, 0)),
        out_shape=jax.ShapeDtypeStruct((m, d), out_dtype),
        compiler_params=_params(("parallel",)),
        name="rmsnorm",
    )(x, g.reshape(1, d).astype(F32))


def _mm_kernel(*refs, nk, seg, w_t, n_out, copy_groups, has_res, has_rope, rope_scale, act, tn,
               has_scale, norm_next):
    it = iter(refs)
    x_refs = [next(it) for _ in seg]
    w_ref = next(it)
    cos_ref = next(it) if has_rope else None
    sin_ref = next(it) if has_rope else None
    res_ref = next(it) if has_res else None
    ss_ref = next(it) if has_scale else None
    g_ref = next(it) if norm_next else None
    copy_in = [[next(it) for _ in range(depth)] for depth in copy_groups]
    o_refs = [next(it) for _ in range(n_out)]
    xg_ref = next(it) if norm_next else None
    ss_out = next(it) if norm_next else None
    copy_out = [next(it) for _ in copy_groups]

    for srcs, dst in zip(copy_in, copy_out):
        for d, src in enumerate(srcs):
            dst[d] = src[...]

    def emit_norm_inputs(x_new):
        xg_ref[...] = (x_new * g_ref[...]).astype(xg_ref.dtype)
        rows = jnp.broadcast_to(jnp.sum(x_new * x_new, axis=1, keepdims=True), ss_out.shape)
        j = pl.program_id(1)

        @pl.when(j == 0)
        def _():
            ss_out[...] = rows

        @pl.when(j > 0)
        def _():
            ss_out[...] += rows

    if nk > 1:
        o = o_refs[0]
        k = pl.program_id(2)
        chunks = [slice(c, c + MXU_COLS) for c in range(0, tn, MXU_COLS)]

        @pl.when(k == 0)
        def _():
            for sl in chunks:
                d = jnp.dot(x_refs[0][...], w_ref[:, sl], preferred_element_type=F32)
                o[:, sl] = res_ref[:, sl] + d if has_res else d

        @pl.when(k > 0)
        def _():
            for sl in chunks:
                o[:, sl] += jnp.dot(x_refs[0][...], w_ref[:, sl], preferred_element_type=F32)

        if norm_next:
            pl.when(k == nk - 1)(lambda: emit_norm_inputs(o[...]))
        return

    if w_t:
        part = lax.dot_general(x_refs[0][...], w_ref[...], (((1,), (1,)), ((), ())),
                               preferred_element_type=F32)
    elif len(seg) == 1:
        part = jnp.dot(x_refs[0][...].astype(BF16), w_ref[...], preferred_element_type=F32)
    else:
        part, off = None, 0
        for x_ref, width in zip(x_refs, seg):
            d = jnp.dot(x_ref[...], w_ref[off:off + width, :], preferred_element_type=F32)
            part = d if part is None else part + d
            off += width

    def store(sl, val):
        for o in o_refs:
            o[:, sl] = val.astype(o.dtype)

    acc = part
    if has_scale:
        acc = acc * lax.rsqrt(ss_ref[:, 0:1] * (1.0 / D_MODEL) + NORM_EPS)
    if has_rope:
        cos = cos_ref[...]
        sin = sin_ref[...]
        for g in range(tn // DH_DA):
            sl = slice(g * DH_DA, (g + 1) * DH_DA)
            store(sl, _rotate(acc[:, sl], cos, sin) * rope_scale)
        return
    if act == "relu2":
        r = jnp.maximum(acc, 0.0)
        acc = r * r
    if has_res:
        acc = res_ref[...] + acc
    store(slice(None), acc)
    if norm_next:
        emit_norm_inputs(acc)


def _matmul(xs, w, *, tm, tn, tk, out_dtypes, n=None, w_col0=0, w_t=False, residual=None, rope=None,
            rope_scale=1.0, act=None, row_ss=None, next_gain=None, stack=(), name="matmul"):
    xs = list(xs) if isinstance(xs, (list, tuple)) else [xs]
    seg = [x.shape[1] for x in xs]
    m = xs[0].shape[0]
    kdim = sum(seg)
    n = w.shape[0 if w_t else 1] - w_col0 if n is None else n
    nk = kdim // tk
    ni, nj = m // tm, n // tn
    assert len(xs) == 1 or (nk == 1 and not w_t)
    if nk > 1:
        assert (not w_t and rope is None and act is None and row_ss is None and not stack
                and tuple(out_dtypes) == (F32,) and xs[0].dtype == BF16)
    jb = w_col0 // tn

    if len(xs) == 1:
        in_specs = [pl.BlockSpec((tm, tk), lambda i, j, k: (i, k))]
    else:
        in_specs = [pl.BlockSpec((tm, s), lambda i, j, k: (i, 0)) for s in seg]
    if w_t:
        in_specs.append(pl.BlockSpec((tn, tk), lambda i, j, k: (j + jb, k)))
    else:
        in_specs.append(pl.BlockSpec((tk, tn), lambda i, j, k: (k, j + jb)))
    args = xs + [w]
    if rope is not None:
        nr = rope[0].shape[0] // tm
        spec = pl.BlockSpec((tm, DH_DA), lambda i, j, k: (i % nr, 0))
        in_specs += [spec, spec]
        args += [rope[0], rope[1]]
    if residual is not None:
        in_specs.append(pl.BlockSpec((tm, tn), lambda i, j, k: (i, j)))
        args.append(residual)
    if row_ss is not None:
        in_specs.append(pl.BlockSpec((tm, LANES), lambda i, j, k: (i, 0)))
        args.append(row_ss)
    if next_gain is not None:
        in_specs.append(pl.BlockSpec((1, tn), lambda i, j, k: (0, j)))
        args.append(next_gain.reshape(1, n).astype(F32))
    out_specs = [pl.BlockSpec((tm, tn), lambda i, j, k: (i, j)) for _ in out_dtypes]
    out_shapes = [jax.ShapeDtypeStruct((m, n), dt) for dt in out_dtypes]
    if next_gain is not None:
        out_specs += [pl.BlockSpec((tm, tn), lambda i, j, k: (i, j)),
                      pl.BlockSpec((tm, LANES), lambda i, j, k: (i, 0))]
        out_shapes += [jax.ShapeDtypeStruct((m, n), BF16), jax.ShapeDtypeStruct((m, LANES), F32)]
    slab = m // (ni * nj)
    for group in stack:
        width = group[0].shape[1]
        for arr in group:
            in_specs.append(pl.BlockSpec((slab, width), lambda i, j, k: (i * nj + j, 0)))
            args.append(arr)
        out_specs.append(pl.BlockSpec((len(group), slab, width), lambda i, j, k: (0, i * nj + j, 0)))
        out_shapes.append(jax.ShapeDtypeStruct((len(group), m, width), group[0].dtype))
    outs = pl.pallas_call(
        functools.partial(_mm_kernel, nk=nk, seg=seg, w_t=w_t, n_out=len(out_dtypes),
                          copy_groups=[len(g) for g in stack],
                          has_res=residual is not None, has_rope=rope is not None, rope_scale=rope_scale,
                          act=act, tn=tn, has_scale=row_ss is not None, norm_next=next_gain is not None),
        grid=(ni, nj, nk),
        in_specs=in_specs,
        out_specs=out_specs,
        out_shape=out_shapes,
        compiler_params=_params(("parallel", "arbitrary", "arbitrary")),
        name=name,
    )(*args)
    return outs[0] if len(outs) == 1 else outs


def _cast_mm_kernel(*refs, nk, has_res, act):
    it = iter(refs)
    x_ref = next(it)
    w_ref = next(it)
    res_ref = next(it) if has_res else None
    wb_ref = next(it)
    y_ref = next(it)
    k = pl.program_id(1)
    wb = w_ref[...].astype(BF16)
    wb_ref[...] = wb
    part = jnp.dot(x_ref[...].astype(BF16), wb, preferred_element_type=F32)

    @pl.when(k == 0)
    def _():
        y_ref[...] = part

    @pl.when(k > 0)
    def _():
        y_ref[...] += part

    if act is not None or has_res:
        @pl.when(k == nk - 1)
        def _():
            acc = y_ref[...]
            if act == "relu2":
                r = jnp.maximum(acc, 0.0)
                acc = r * r
            if has_res:
                acc = res_ref[...] + acc
            y_ref[...] = acc


def _cast_matmul(x, w_all, layer, n, *, tn, tk, residual=None, act=None, name="cast_matmul"):
    m, kdim = x.shape
    nk = kdim // tk
    in_specs = [pl.BlockSpec((m, tk), lambda j, k: (0, k)),
                pl.BlockSpec((None, tk, tn), lambda j, k: (layer, k, j))]
    args = [x, w_all]
    if residual is not None:
        in_specs.append(pl.BlockSpec((m, tn), lambda j, k: (0, j)))
        args.append(residual)
    wb, y = pl.pallas_call(
        functools.partial(_cast_mm_kernel, nk=nk, has_res=residual is not None, act=act),
        grid=(n // tn, nk),
        in_specs=in_specs,
        out_specs=[pl.BlockSpec((tk, tn), lambda j, k: (k, j)),
                   pl.BlockSpec((m, tn), lambda j, k: (0, j))],
        out_shape=[jax.ShapeDtypeStruct((kdim, n), BF16), jax.ShapeDtypeStruct((m, n), F32)],
        compiler_params=_params(("parallel", "arbitrary")),
        name=name,
    )(*args)
    return y, wb


def _nt_dot(x, wt):
    return lax.dot_general(x, wt, (((1,), (1,)), ((), ())), preferred_element_type=F32)


def _cast_mm_t_kernel(x_ref, w_ref, wb_ref, y_ref):
    wb = w_ref[...].astype(BF16)
    wb_ref[...] = wb
    y_ref[...] = _nt_dot(x_ref[...].astype(BF16), wb)


def _cast_matmul_t(x, wt_all, layer, n, *, tn, name):
    m, kdim = x.shape
    wb, y = pl.pallas_call(
        _cast_mm_t_kernel,
        grid=(n // tn,),
        in_specs=[pl.BlockSpec((m, kdim), lambda j: (0, 0)),
                  pl.BlockSpec((None, tn, kdim), lambda j: (layer, j, 0))],
        out_specs=[pl.BlockSpec((tn, kdim), lambda j: (j, 0)),
                   pl.BlockSpec((m, tn), lambda j: (0, j))],
        out_shape=[jax.ShapeDtypeStruct((n, kdim), BF16), jax.ShapeDtypeStruct((m, n), F32)],
        compiler_params=_params(("parallel",)),
        name=name,
    )(x, wt_all)
    return y, wb


def _cast_tail_kernel(x_ref, a_ref, b_ref, wr_ref, wg_ref, yr_ref, yg_ref):
    j = pl.program_id(0)
    xb = x_ref[...].astype(BF16)
    a = a_ref[...]
    tile = jnp.concatenate([a[SUBLANES:], b_ref[...]], axis=0).astype(BF16)
    wr_ref[...] = tile
    yr_ref[...] = _nt_dot(xb, tile)

    @pl.when(j == 0)
    def _():
        pad = jnp.zeros((LANES - SUBLANES, a.shape[1]), F32)
        g = jnp.concatenate([a[:SUBLANES], pad], axis=0).astype(BF16)
        wg_ref[...] = g
        yg_ref[...] = _nt_dot(xb, g)


def _cast_tail(x, wt_all, layer, *, tn, name):
    assert 2 * H_M == SUBLANES
    m, kdim = x.shape
    n = 2 * D_RG
    blk0 = D_MAIN // tn
    return pl.pallas_call(
        _cast_tail_kernel,
        grid=(n // tn,),
        in_specs=[pl.BlockSpec((m, kdim), lambda j: (0, 0)),
                  pl.BlockSpec((None, tn, kdim), lambda j: (layer, blk0 + j, 0)),
                  pl.BlockSpec((None, SUBLANES, kdim), lambda j: (layer, (blk0 + j + 1) * (tn // SUBLANES), 0))],
        out_specs=[pl.BlockSpec((tn, kdim), lambda j: (j, 0)),
                   pl.BlockSpec((LANES, kdim), lambda j: (0, 0)),
                   pl.BlockSpec((m, tn), lambda j: (0, j)),
                   pl.BlockSpec((m, LANES), lambda j: (0, 0))],
        out_shape=[jax.ShapeDtypeStruct((n, kdim), BF16), jax.ShapeDtypeStruct((LANES, kdim), BF16),
                   jax.ShapeDtypeStruct((m, n), F32), jax.ShapeDtypeStruct((m, LANES), F32)],
        compiler_params=_params(("arbitrary",)),
        name=name,
    )(x, wt_all, wt_all)


def _rope_rows_kernel(x_ref, cos_ref, sin_ref, o_ref, *, n_q):
    cos = cos_ref[...]
    sin = sin_ref[...]
    for g in range(x_ref.shape[1] // DH_DA):
        sl = slice(g * DH_DA, (g + 1) * DH_DA)
        y = _rotate(x_ref[:, sl], cos, sin)
        o_ref[:, sl] = y * ATTN_QSCALE if g < n_q else y


def _rope_rows(x, cos, sin):
    m, n = x.shape
    full = lambda c: pl.BlockSpec((m, c), lambda i: (0, 0))
    return pl.pallas_call(
        functools.partial(_rope_rows_kernel, n_q=D_ATTN // DH_DA),
        grid=(1,),
        in_specs=[full(n), full(DH_DA), full(DH_DA)],
        out_specs=full(n),
        out_shape=jax.ShapeDtypeStruct((m, n), F32),
        compiler_params=_params(("arbitrary",)),
        name="rope_sample",
    )(x, cos, sin)


def _attn_prompt_kernel(lq1, lk1, lq2, lk2, g_ref, q_ref, k_ref, v_ref, o_ref, *, t, nq, hp, lam_init):
    qi = pl.program_id(2)
    dh2 = 2 * DH_DA
    lam = _lambda(lq1, lk1, lq2, lk2, lam_init)
    tri = (lax.broadcasted_iota(jnp.int32, (t, t), 1) <= lax.broadcasted_iota(jnp.int32, (t, t), 0))

    def head(n_below, h):
        spans = ([(0, n_below * t, False)] if n_below else []) + [(n_below * t, (n_below + 1) * t, True)]
        probs, inv = [], []
        for c in range(2):
            sl = slice(h * dh2 + c * DH_DA, h * dh2 + (c + 1) * DH_DA)
            ss = []
            for lo, hi, diagonal in spans:
                s = lax.dot_general(q_ref[:, sl], k_ref[lo:hi, sl], (((1,), (1,)), ((), ())),
                                    preferred_element_type=F32)
                ss.append(jnp.where(tri, s, NEG) if diagonal else s)
            m = functools.reduce(jnp.maximum, [jnp.max(s, axis=1, keepdims=True) for s in ss])
            ps = [jnp.exp2(s - m) for s in ss]
            probs.append(ps)
            inv.append(1.0 / functools.reduce(jnp.add, [jnp.sum(p, axis=1, keepdims=True) for p in ps]))
        r1 = inv[0]
        r2 = lam * inv[1]
        hs = slice(h * dh2, (h + 1) * dh2)
        out = None
        for i, (lo, hi, _) in enumerate(spans):
            a = (probs[0][i] * r1 - probs[1][i] * r2).astype(BF16)
            d = jnp.dot(a, v_ref[lo:hi, hs], preferred_element_type=F32)
            out = d if out is None else out + d
        ms = jnp.mean(out * out, axis=-1, keepdims=True)
        y = out * lax.rsqrt(ms + SUBLN_EPS) * g_ref[...]
        o_ref[:, hs] = (y * (1.0 - lam_init)).astype(o_ref.dtype)

    def block(n_below):
        for h in range(hp):
            head(n_below, h)

    for n_below in range(nq):
        pl.when(qi == n_below)(functools.partial(block, n_below))


def _attn_prompt(q, k, v, lam_rows, subln_g, lam_init, batch, seq, t=512, hp=2):
    nq = seq // t
    dh2 = 2 * DH_DA
    row_spec = pl.BlockSpec((1, DH_DA), lambda b, h, i: (0, 0))
    return pl.pallas_call(
        functools.partial(_attn_prompt_kernel, t=t, nq=nq, hp=hp, lam_init=lam_init),
        grid=(batch, H_DA // hp, nq),
        in_specs=[row_spec, row_spec, row_spec, row_spec,
                  pl.BlockSpec((1, dh2), lambda b, h, i: (0, 0)),
                  pl.BlockSpec((t, hp * dh2), lambda b, h, i: (b * nq + i, h)),
                  pl.BlockSpec((seq, hp * dh2), lambda b, h, i: (b, h)),
                  pl.BlockSpec((seq, hp * dh2), lambda b, h, i: (b, h))],
        out_specs=pl.BlockSpec((t, hp * dh2), lambda b, h, i: (b * nq + i, h)),
        out_shape=jax.ShapeDtypeStruct((batch * seq, D_ATTN), BF16),
        compiler_params=_params(("parallel", "parallel", "arbitrary")),
        name="attn_prompt",
    )(*lam_rows, subln_g.reshape(1, dh2), q, k, v)


def _attn_sample_kernel(*refs, lam_init, n_steps, group):
    pt_ref, lq1, lk1, lq2, lk2, g_ref, q_ref, kn_ref, vn_ref = refs[:9]
    k_refs = refs[9:9 + group]
    v_refs = refs[9 + group:9 + 2 * group]
    o_ref, m_sc, l_sc, acc_sc = refs[9 + 2 * group:]
    del pt_ref
    step = pl.program_id(1)
    rows = PAGE_SIZE * H_DA

    @pl.when(step == 0)
    def _():
        m_sc[...] = jnp.full(m_sc.shape, NEG, F32)
        l_sc[...] = jnp.zeros(l_sc.shape, F32)
        acc_sc[...] = jnp.zeros(acc_sc.shape, F32)

    q = q_ref[...]
    zero = jnp.zeros((H_DA, DH_DA), F32)
    qbd = jnp.concatenate([jnp.concatenate([q[:, :DH_DA], zero], axis=1),
                           jnp.concatenate([zero, q[:, DH_DA:]], axis=1)], axis=0).astype(BF16)
    own_head = (lax.broadcasted_iota(jnp.int32, (2 * H_DA, rows), 0) % H_DA
                == lax.broadcasted_iota(jnp.int32, (2 * H_DA, rows), 1) % H_DA)
    scores = []
    for g in range(group):
        k2 = k_refs[g][...].reshape(rows, 2 * DH_DA).astype(BF16)
        s = lax.dot_general(qbd, k2, (((1,), (1,)), ((), ())), preferred_element_type=F32)
        scores.append(jnp.where(own_head, s, NEG))
    m_prev = m_sc[...]
    m_new = m_prev
    for s in scores:
        m_new = jnp.maximum(m_new, jnp.max(s, axis=1, keepdims=True))
    alpha = jnp.exp2(m_prev - m_new)
    l_new = alpha * l_sc[...]
    acc = alpha * acc_sc[...]
    for g in range(group):
        p = jnp.exp2(scores[g] - m_new)
        l_new = l_new + jnp.sum(p, axis=1, keepdims=True)
        v2 = v_refs[g][...].reshape(rows, 2 * DH_DA).astype(BF16)
        acc = acc + jnp.dot(p.astype(BF16), v2, preferred_element_type=F32)
    m_sc[...] = m_new
    l_sc[...] = l_new
    acc_sc[...] = acc

    @pl.when(step == n_steps - 1)
    def _():
        kn = kn_ref[...]
        vn = vn_ref[...]
        pn = kn * q
        sn = jnp.concatenate([jnp.sum(pn[:, :DH_DA], axis=-1, keepdims=True),
                              jnp.sum(pn[:, DH_DA:], axis=-1, keepdims=True)], axis=0)
        m_fin = jnp.maximum(m_new, sn)
        pe = jnp.exp2(sn - m_fin)
        a_fin = jnp.exp2(m_new - m_fin)
        l_fin = a_fin * l_new + pe
        outs = (a_fin * acc + pe * jnp.concatenate([vn, vn], axis=0)) / l_fin
        lam = _lambda(lq1, lk1, lq2, lk2, lam_init)
        out = outs[0:H_DA] - lam * outs[H_DA:]
        ms = jnp.mean(out * out, axis=-1, keepdims=True)
        y = out * lax.rsqrt(ms + SUBLN_EPS) * g_ref[...]
        o_ref[...] = (y * (1.0 - lam_init)).astype(o_ref.dtype)


def _attn_sample(page_table, q, k_new, v_new, cache_k, cache_v, layer, lam_rows, subln_g, lam_init, group=8):
    bd, n_pages = page_table.shape
    n_steps = n_pages // group
    dh2 = 2 * DH_DA
    row_spec = pl.BlockSpec((1, DH_DA), lambda b, p, pt: (0, 0))
    tok_spec = pl.BlockSpec((None, H_DA, dh2), lambda b, p, pt: (b, 0, 0))

    def page_spec(g):
        return pl.BlockSpec((None, None, PAGE_SIZE, H_DA, dh2),
                            lambda b, p, pt: (layer, pt[b, p * group + g], 0, 0, 0))

    pages = [page_spec(g) for g in range(group)]
    grid_spec = pltpu.PrefetchScalarGridSpec(
        num_scalar_prefetch=1,
        grid=(bd, n_steps),
        in_specs=[row_spec, row_spec, row_spec, row_spec,
                  pl.BlockSpec((1, dh2), lambda b, p, pt: (0, 0)),
                  tok_spec, tok_spec, tok_spec] + pages + pages,
        out_specs=tok_spec,
        scratch_shapes=[pltpu.VMEM((2 * H_DA, 1), F32), pltpu.VMEM((2 * H_DA, 1), F32),
                        pltpu.VMEM((2 * H_DA, dh2), F32)],
    )
    return pl.pallas_call(
        functools.partial(_attn_sample_kernel, lam_init=lam_init, n_steps=n_steps, group=group),
        grid_spec=grid_spec,
        out_shape=jax.ShapeDtypeStruct((bd, H_DA, dh2), F32),
        compiler_params=_params(("parallel", "arbitrary")),
        name="attn_sample",
    )(page_table, *lam_rows, subln_g.reshape(1, dh2), q, k_new, v_new,
      *([cache_k] * group), *([cache_v] * group))


def _mlstm_out(h, g, om):
    mu = jnp.mean(h, axis=-1, keepdims=True)
    hc = h - mu
    var = jnp.mean(hc * hc, axis=-1, keepdims=True)
    return hc * lax.rsqrt(var + NORM_EPS) * g * _sigmoid(om)


def _mlstm_prompt_kernel(big_ref, bfg_ref, q_ref, k_ref, v_ref, om_ref, gc_ref, gr_ref, ng_ref,
                         hm_ref, c_out, n_out, m_out, c_sc, n_sc, m_sc, *, n_chunks):
    ci = pl.program_id(1)
    L = MLSTM_CHUNK

    @pl.when(ci == 0)
    def _():
        c_sc[...] = jnp.zeros(c_sc.shape, F32)
        n_sc[...] = jnp.zeros(n_sc.shape, F32)
        m_sc[...] = jnp.zeros(m_sc.shape, F32)

    ri = lax.broadcasted_iota(jnp.int32, (L, L), 0)
    cj = lax.broadcasted_iota(jnp.int32, (L, L), 1)
    lower = ri >= cj
    gc = gc_ref[...]
    gr = gr_ref[...]
    for h in range(H_M):
        hs = slice(h * DH_M, (h + 1) * DH_M)
        ig_c = gc[:, h:h + 1] + big_ref[h]
        lf_c = _log_sigmoid(gc[:, H_M + h:H_M + h + 1] + bfg_ref[h])
        ig_r = gr[h:h + 1, :] + big_ref[h]
        lf_r = _log_sigmoid(gr[H_M + h:H_M + h + 1, :] + bfg_ref[h])
        b_col = jnp.sum(jnp.where(lower, lf_r, 0.0), axis=1, keepdims=True)
        b_row = jnp.sum(jnp.where(cj >= ri, lf_c, 0.0), axis=0, keepdims=True)
        dm = jnp.where(lower, b_col - b_row + ig_r, -jnp.inf)
        m_prev = m_sc[h:h + 1, 0:1]
        inter = b_col + m_prev
        m_t = jnp.maximum(inter, jnp.max(dm, axis=1, keepdims=True))
        w_intra = jnp.exp(dm - m_t)
        w_inter = jnp.exp(inter - m_t)
        q = q_ref[:, hs]
        ks = k_ref[:, hs] * (DH_M ** -0.5)
        v = v_ref[:, hs]
        c_prev = c_sc[h]
        n_prev = n_sc[h:h + 1, :]
        sc = w_intra * lax.dot_general(q, ks, (((1,), (1,)), ((), ())), preferred_element_type=F32)
        num = (jnp.dot(sc.astype(BF16), v, preferred_element_type=F32)
               + w_inter * lax.dot_general(q, c_prev.astype(BF16), (((1,), (1,)), ((), ())),
                                           preferred_element_type=F32))
        den = (jnp.sum(sc, axis=1, keepdims=True)
               + w_inter * jnp.sum(q.astype(F32) * n_prev, axis=1, keepdims=True))
        hh = num / jnp.maximum(jnp.abs(den), jnp.exp(-m_t))
        m_new = m_t[L - 1:L, :]
        decay = w_inter[L - 1:L, :]
        wk = jnp.exp(b_col[L - 1:L, :] - b_col + ig_c - m_new)
        wv = (wk * v.astype(F32)).astype(BF16)
        c_sc[h] = decay * c_prev + lax.dot_general(wv, ks, (((0,), (0,)), ((), ())),
                                                   preferred_element_type=F32)
        n_sc[h:h + 1, :] = decay * n_prev + jnp.sum(wk * ks.astype(F32), axis=0, keepdims=True)
        m_sc[h:h + 1, :] = jnp.broadcast_to(m_new, (1, LANES))
        hm_ref[:, hs] = _mlstm_out(hh, ng_ref[:, hs], om_ref[:, hs].astype(F32)).astype(hm_ref.dtype)

    @pl.when(ci == n_chunks - 1)
    def _():
        c_out[...] = c_sc[...]
        n_out[...] = n_sc[0:H_M, :]
        m_out[...] = m_sc[0:H_M, :]


def _mlstm_prompt(um, gates, gates_t, b_ig, b_fg, norm_g, batch, seq):
    L = MLSTM_CHUNK
    nc = seq // L
    smem = pl.BlockSpec(memory_space=pltpu.SMEM)

    def col(cb):
        return pl.BlockSpec((L, D_MLSTM), lambda b, c: (b * nc + c, cb))

    return pl.pallas_call(
        functools.partial(_mlstm_prompt_kernel, n_chunks=nc),
        grid=(batch, nc),
        in_specs=[smem, smem, col(0), col(1), col(2), col(3),
                  pl.BlockSpec((L, LANES), lambda b, c: (b * nc + c, 0)),
                  pl.BlockSpec((None, SUBLANES, L), lambda b, c: (b, 0, c)),
                  pl.BlockSpec((1, D_MLSTM), lambda b, c: (0, 0))],
        out_specs=[pl.BlockSpec((L, D_MLSTM), lambda b, c: (b * nc + c, 0)),
                   pl.BlockSpec((None, H_M, DH_M, DH_M), lambda b, c: (b, 0, 0, 0)),
                   pl.BlockSpec((None, H_M, DH_M), lambda b, c: (b, 0, 0)),
                   pl.BlockSpec((None, H_M, LANES), lambda b, c: (b, 0, 0))],
        out_shape=[jax.ShapeDtypeStruct((batch * seq, D_MLSTM), BF16),
                   jax.ShapeDtypeStruct((batch, H_M, DH_M, DH_M), F32),
                   jax.ShapeDtypeStruct((batch, H_M, DH_M), F32),
                   jax.ShapeDtypeStruct((batch, H_M, LANES), F32)],
        scratch_shapes=[pltpu.VMEM((H_M, DH_M, DH_M), F32), pltpu.VMEM((SUBLANES, DH_M), F32),
                        pltpu.VMEM((SUBLANES, LANES), F32)],
        compiler_params=_params(("parallel", "arbitrary")),
        name="mlstm_prompt",
    )(b_ig, b_fg, um, um, um, um, gates, gates_t, norm_g.reshape(1, D_MLSTM))


def _mlstm_sample_kernel(big_ref, bfg_ref, u_ref, vcol_ref, g_ref, ng_ref, c_ref, n_ref, m_ref,
                         hm_ref, c_out, n_out, m_out):
    u = u_ref[...]
    g = g_ref[...]
    for h in range(H_M):
        def part(i):
            return u[:, i * D_MLSTM + h * DH_M:i * D_MLSTM + (h + 1) * DH_M]
        q, k, v, om = part(0), part(1), part(2), part(3)
        ks = k * (DH_M ** -0.5)
        ig = g[:, h:h + 1] + big_ref[h]
        lf = _log_sigmoid(g[:, H_M + h:H_M + h + 1] + bfg_ref[h])
        m_prev = m_ref[:, h:h + 1]
        inter = lf + m_prev
        m_t = jnp.maximum(inter, ig)
        w_intra = jnp.exp(ig - m_t)
        w_inter = jnp.exp(inter - m_t)
        c_prev = c_ref[h]
        n_prev = n_ref[h:h + 1, :]
        q8 = jnp.broadcast_to(q, (2 * SUBLANES, DH_M)).astype(BF16)
        cq = lax.dot_general(q8, c_prev.astype(BF16), (((1,), (1,)), ((), ())),
                             preferred_element_type=F32)[0:1, :]
        sc = w_intra * jnp.sum(q * ks, axis=-1, keepdims=True)
        num = sc * v + w_inter * cq
        den = sc + w_inter * jnp.sum(n_prev * q, axis=-1, keepdims=True)
        hh = num / jnp.maximum(jnp.abs(den), jnp.exp(-m_t))
        c_out[h] = w_inter * c_prev + (w_intra * vcol_ref[h]) * ks
        n_out[h:h + 1, :] = w_inter * n_prev + w_intra * ks
        m_out[h:h + 1, :] = jnp.broadcast_to(m_t, (1, LANES))
        hs = slice(h * DH_M, (h + 1) * DH_M)
        hm_ref[:, hs] = _mlstm_out(hh, ng_ref[:, hs], om)


def _mlstm_sample(um, gates, b_ig, b_fg, norm_g, c_state, n_state, m_state):
    bd = um.shape[0]
    smem = pl.BlockSpec(memory_space=pltpu.SMEM)
    u4 = um.reshape(bd, 1, 4 * D_MLSTM)
    vcol = um[:, 2 * D_MLSTM:3 * D_MLSTM].reshape(bd, H_M, DH_M, 1)
    return pl.pallas_call(
        _mlstm_sample_kernel,
        grid=(bd,),
        in_specs=[smem, smem,
                  pl.BlockSpec((None, 1, 4 * D_MLSTM), lambda b: (b, 0, 0)),
                  pl.BlockSpec((None, H_M, DH_M, 1), lambda b: (b, 0, 0, 0)),
                  pl.BlockSpec((None, 1, LANES), lambda b: (b, 0, 0)),
                  pl.BlockSpec((1, D_MLSTM), lambda b: (0, 0)),
                  pl.BlockSpec((None, H_M, DH_M, DH_M), lambda b: (b, 0, 0, 0)),
                  pl.BlockSpec((None, H_M, DH_M), lambda b: (b, 0, 0)),
                  pl.BlockSpec((None, 1, H_M), lambda b: (b, 0, 0))],
        out_specs=[pl.BlockSpec((None, 1, D_MLSTM), lambda b: (b, 0, 0)),
                   pl.BlockSpec((None, H_M, DH_M, DH_M), lambda b: (b, 0, 0, 0)),
                   pl.BlockSpec((None, H_M, DH_M), lambda b: (b, 0, 0)),
                   pl.BlockSpec((None, H_M, LANES), lambda b: (b, 0, 0))],
        out_shape=[jax.ShapeDtypeStruct((bd, 1, D_MLSTM), F32),
                   jax.ShapeDtypeStruct((bd, H_M, DH_M, DH_M), F32),
                   jax.ShapeDtypeStruct((bd, H_M, DH_M), F32),
                   jax.ShapeDtypeStruct((bd, H_M, LANES), F32)],
        compiler_params=_params(("parallel",)),
        name="mlstm_sample",
    )(b_ig, b_fg, u4, vcol, gates.reshape(bd, 1, LANES), norm_g.reshape(1, D_MLSTM),
      c_state, n_state, m_state.reshape(bd, 1, H_M))


def _rglru_coeffs(xc, wra_ref, bra_ref, wrx_ref, brx_ref, lam_ref):
    rs, is_ = [], []
    for n in range(RG_BLOCKS):
        xb = xc[:, n * RG_BW:(n + 1) * RG_BW].astype(BF16)
        rs.append(jnp.dot(xb, wra_ref[n], preferred_element_type=F32))
        is_.append(jnp.dot(xb, wrx_ref[n], preferred_element_type=F32))
    r = _sigmoid(jnp.concatenate(rs, axis=-1) + bra_ref[...])
    i = _sigmoid(jnp.concatenate(is_, axis=-1) + brx_ref[...])
    log_a = -RG_C * r * _softplus(-lam_ref[...])
    a = jnp.exp(log_a)
    mult = jnp.sqrt(1.0 - a * a)
    return a, mult * i * xc


def _rglru_prompt_kernel(x_ref, gate_ref, cw_ref, cb_ref, wra_ref, bra_ref, wrx_ref, brx_ref, lam_ref,
                         y_ref, conv_out, h_out, xbuf, a_sc, b_sc, h_sc, *, tt, n_tiles):
    ti = pl.program_id(1)
    pad = SUBLANES

    @pl.when(ti == 0)
    def _():
        xbuf[0:pad, :] = jnp.zeros((pad, D_RG), F32)
        h_sc[...] = jnp.zeros(h_sc.shape, F32)

    xbuf[pad:pad + tt, :] = x_ref[...].astype(F32)
    xc = cb_ref[...] + sum(xbuf[pad - (CONV_W - 1) + j:pad - (CONV_W - 1) + j + tt, :] * cw_ref[j:j + 1, :]
                           for j in range(CONV_W))
    a, b = _rglru_coeffs(xc, wra_ref, bra_ref, wrx_ref, brx_ref, lam_ref)
    a_sc[...] = a
    b_sc[...] = b

    def step(t, h):
        h = a_sc[pl.ds(t, 1), :] * h + b_sc[pl.ds(t, 1), :]
        b_sc[pl.ds(t, 1), :] = h
        return h

    h_last = lax.fori_loop(0, tt, step, h_sc[...], unroll=8)
    h_sc[...] = h_last
    y_ref[...] = (b_sc[...] * _gelu_tanh(gate_ref[...].astype(F32))).astype(y_ref.dtype)
    tail = xbuf[tt:tt + pad, :]
    xbuf[0:pad, :] = tail

    @pl.when(ti == n_tiles - 1)
    def _():
        conv_out[...] = tail[pad - (CONV_W - 1):, :]
        h_out[...] = h_last


def _rglru_weights(conv_w, conv_b, w_ra, b_ra, w_rx, b_rx, lam):
    row = lambda z: z.reshape(1, D_RG).astype(F32)
    return (conv_w.astype(F32), row(conv_b), w_ra.astype(BF16), row(b_ra), w_rx.astype(BF16), row(b_rx), row(lam))


def _rglru_weight_specs():
    zeros2 = (lambda *a: (0, 0))
    zeros3 = (lambda *a: (0, 0, 0))
    return [pl.BlockSpec((CONV_W, D_RG), zeros2), pl.BlockSpec((1, D_RG), zeros2),
            pl.BlockSpec((RG_BLOCKS, RG_BW, RG_BW), zeros3), pl.BlockSpec((1, D_RG), zeros2),
            pl.BlockSpec((RG_BLOCKS, RG_BW, RG_BW), zeros3), pl.BlockSpec((1, D_RG), zeros2),
            pl.BlockSpec((1, D_RG), zeros2)]


def _rglru_prompt(urg, weights, batch, seq, tt=256):
    nt = seq // tt
    return pl.pallas_call(
        functools.partial(_rglru_prompt_kernel, tt=tt, n_tiles=nt),
        grid=(batch, nt),
        in_specs=[pl.BlockSpec((tt, D_RG), lambda b, t: (b * nt + t, 0)),
                  pl.BlockSpec((tt, D_RG), lambda b, t: (b * nt + t, 1))] + _rglru_weight_specs(),
        out_specs=[pl.BlockSpec((tt, D_RG), lambda b, t: (b * nt + t, 0)),
                   pl.BlockSpec((None, CONV_W - 1, D_RG), lambda b, t: (b, 0, 0)),
                   pl.BlockSpec((None, 1, D_RG), lambda b, t: (b, 0, 0))],
        out_shape=[jax.ShapeDtypeStruct((batch * seq, D_RG), BF16),
                   jax.ShapeDtypeStruct((batch, CONV_W - 1, D_RG), F32),
                   jax.ShapeDtypeStruct((batch, 1, D_RG), F32)],
        scratch_shapes=[pltpu.VMEM((tt + SUBLANES, D_RG), F32), pltpu.VMEM((tt, D_RG), F32),
                        pltpu.VMEM((tt, D_RG), F32), pltpu.VMEM((1, D_RG), F32)],
        compiler_params=_params(("parallel", "arbitrary")),
        name="rglru_prompt",
    )(urg, urg, *weights)


def _rglru_sample_kernel(x_ref, gate_ref, p0_ref, p1_ref, p2_ref, h0_ref, cw_ref, cb_ref, wra_ref, bra_ref,
                         wrx_ref, brx_ref, lam_ref, y_ref, h_out):
    x = x_ref[...]
    xc = cb_ref[...] + (p0_ref[...] * cw_ref[0:1, :] + p1_ref[...] * cw_ref[1:2, :]
                        + p2_ref[...] * cw_ref[2:3, :] + x * cw_ref[3:4, :])
    a, b = _rglru_coeffs(xc, wra_ref, bra_ref, wrx_ref, brx_ref, lam_ref)
    h = a * h0_ref[...] + b
    h_out[...] = h
    y_ref[...] = h * _gelu_tanh(gate_ref[...])


def _rglru_sample(x, gate, conv_prev, h0, weights):
    bd = x.shape[0]
    full = pl.BlockSpec((bd, D_RG), lambda i: (0, 0))
    return pl.pallas_call(
        _rglru_sample_kernel,
        grid=(1,),
        in_specs=[full] * 6 + _rglru_weight_specs(),
        out_specs=[full, full],
        out_shape=[jax.ShapeDtypeStruct((bd, D_RG), F32)] * 2,
        compiler_params=_params(("arbitrary",)),
        name="rglru_sample",
    )(x, gate, conv_prev[:, 0], conv_prev[:, 1], conv_prev[:, 2], h0, *weights)


def _rope_tables(pos):
    inv = 1.0 / (ROPE_THETA ** (jnp.arange(0, DH_DA, 2, dtype=F32) / DH_DA))
    ang = pos.astype(F32)[:, None] * inv[None, :]
    ang = jnp.concatenate([ang, ang], axis=-1)
    sign = jnp.concatenate([-jnp.ones((DH_DA // 2,), F32), jnp.ones((DH_DA // 2,), F32)])
    return jnp.cos(ang), jnp.sin(ang) * sign[None, :]


def kernel(x_prompt, x_sample, cache_k, cache_v, page_table, state_mlstm_c, state_mlstm_n, state_mlstm_m, state_conv, state_rglru_h, norm_mix_g, w_in, w_out, lam_q1, lam_k1, lam_q2, lam_k2, attn_subln_g, b_ig, b_fg, mlstm_norm_g, conv_w, conv_b, w_ra, b_ra, w_rx, b_rx, rg_lambda, norm_mlp_g, w_up, w_down, final_norm_g):
    bp, sp, _ = x_prompt.shape
    bd, td, _ = x_sample.shape
    assert td == 1
    past = page_table.shape[1] * PAGE_SIZE
    mp = bp * sp
    dh2 = 2 * DH_DA
    cos_p, sin_p = _rope_tables(jnp.arange(sp))
    cos_s, sin_s = _rope_tables(jnp.full((bd,), past))

    TM = 1024
    TNC = 512

    xp = x_prompt.reshape(mp, D_MODEL)
    xs = x_sample.reshape(bd, D_MODEL)
    w_in_t = jnp.swapaxes(w_in, 1, 2)
    k_layers, v_layers = [], []
    outs = {n: [] for n in ("ks", "vs", "cp", "np", "mp", "cs", "ns", "ms", "cvp", "cvs", "hp", "hs")}
    for l in range(DEPTH):
        lam_init = 0.8 - 0.6 * math.exp(-0.3 * l)
        lam_rows = [z[l].reshape(1, DH_DA).astype(F32) for z in (lam_q1, lam_k1, lam_q2, lam_k2)]
        rg_w = _rglru_weights(conv_w[l], conv_b[l], w_ra[l], b_ra[l], w_rx[l], b_rx[l], rg_lambda[l])

        hs_n = _rmsnorm(xs, norm_mix_g[l], NORM_EPS, F32, tm=bd)
        u_s, w_in_b = _cast_matmul_t(hs_n, w_in_t, l, D_MAIN, tn=TNC, name="s_proj_in")
        wr, wg, urg_s, gates_s = _cast_tail(hs_n, w_in_t, l, tn=TNC, name="s_proj_tail")
        qk_s = _rope_rows(u_s[:, :2 * D_ATTN], cos_s, sin_s)
        q_s, k_s = qk_s[:, :D_ATTN], qk_s[:, D_ATTN:]
        v_s = u_s[:, 2 * D_ATTN:3 * D_ATTN]
        um_s = u_s[:, 3 * D_ATTN:]

        if l == 0:
            hn, ss = _rmsnorm(xp, norm_mix_g[l], NORM_EPS, BF16, tm=512), None
        mm = functools.partial(_matmul, hn, tm=TM, tk=D_MODEL, w_t=True, row_ss=ss)
        q = mm(w_in_b, n=D_ATTN, w_col0=0, tn=1024, out_dtypes=(BF16,), rope=(cos_p, sin_p),
               rope_scale=ATTN_QSCALE, name="proj_q")
        k32, k16 = mm(w_in_b, n=D_ATTN, w_col0=D_ATTN, tn=512, out_dtypes=(F32, BF16), rope=(cos_p, sin_p),
                      name="proj_k")
        v32, v16 = mm(w_in_b, n=D_ATTN, w_col0=2 * D_ATTN, tn=512, out_dtypes=(F32, BF16), name="proj_v")
        k_layers.append(k32)
        v_layers.append(v32)
        um = mm(w_in_b, n=4 * D_MLSTM, w_col0=3 * D_ATTN, tn=1024, out_dtypes=(BF16,), name="proj_m")
        urg = mm(wr, tn=1024, out_dtypes=(BF16,), name="proj_r")
        gates = mm(wg, tn=LANES, out_dtypes=(F32,), name="proj_g")
        gates_t = gates[:, :SUBLANES].reshape(bp, sp, SUBLANES).transpose(0, 2, 1)
        att = _attn_prompt(q, k16, v16, lam_rows, attn_subln_g[l], lam_init, bp, sp)
        hm, c_p, n_p, m_p = _mlstm_prompt(um, gates, gates_t, b_ig[l], b_fg[l], mlstm_norm_g[l], bp, sp)
        hr, cv_p, h_p = _rglru_prompt(urg, rg_w, bp, sp)

        att_s = _attn_sample(page_table, q_s.reshape(bd, H_DA, dh2), k_s.reshape(bd, H_DA, dh2),
                             v_s.reshape(bd, H_DA, dh2), cache_k, cache_v, l, lam_rows,
                             attn_subln_g[l], lam_init)
        hm_s, c_s, n_s, m_s = _mlstm_sample(um_s, gates_s, b_ig[l], b_fg[l], mlstm_norm_g[l],
                                            state_mlstm_c[l], state_mlstm_n[l], state_mlstm_m[l])
        x_rg = urg_s[:, :D_RG]
        hr_s, h_s = _rglru_sample(x_rg, urg_s[:, D_RG:], state_conv[l], state_rglru_h[l], rg_w)
        cat_s = jnp.concatenate([att_s.reshape(bd, D_ATTN), hm_s.reshape(bd, D_MLSTM), hr_s], axis=1)
        x1s, wo = _cast_matmul(cat_s, w_out, l, D_MODEL, tn=TNC, tk=D_MODEL, residual=xs, name="s_proj_out")
        hs_n2 = _rmsnorm(x1s, norm_mlp_g[l], NORM_EPS, F32, tm=bd)
        act_s, wu = _cast_matmul(hs_n2, w_up, l, D_FF, tn=TNC, tk=D_MODEL, act="relu2", name="s_mlp_up")
        xs, wd = _cast_matmul(act_s, w_down, l, D_MODEL, tn=TNC, tk=D_MODEL, residual=x1s, name="s_mlp_down")

        x1, xg1, ss1 = _matmul([att, hm, hr], wo, tm=TM, tn=512, tk=D_MODEL, out_dtypes=(F32,), residual=xp,
                               next_gain=norm_mlp_g[l], name="proj_out")
        up = functools.partial(_matmul, xg1, wu, tm=TM, tn=1024, tk=D_MODEL, out_dtypes=(BF16,), act="relu2",
                               row_ss=ss1, name="mlp_up")
        down = functools.partial(_matmul, w=wd, tm=TM, tn=1024, tk=D_MODEL, out_dtypes=(F32,), residual=x1,
                                 name="mlp_down")
        if l + 1 < DEPTH:
            xp, hn, ss = down(up(), next_gain=norm_mix_g[l + 1])
        else:
            act, k_all, v_all = up(stack=(k_layers, v_layers))
            xp = down(act)

        outs["ks"].append(k_s.reshape(bd, td, H_DA, dh2))
        outs["vs"].append(v_s.reshape(bd, td, H_DA, dh2))
        outs["cp"].append(c_p)
        outs["np"].append(n_p)
        outs["mp"].append(m_p[:, :, 0])
        outs["cs"].append(c_s)
        outs["ns"].append(n_s)
        outs["ms"].append(m_s[:, :, 0])
        outs["cvp"].append(cv_p)
        outs["cvs"].append(jnp.concatenate([state_conv[l][:, 1:], x_rg[:, None, :]], axis=1))
        outs["hp"].append(h_p.reshape(bp, D_RG))
        outs["hs"].append(h_s)

    y_prompt = _rmsnorm(xp, final_norm_g, NORM_EPS, F32, tm=512).reshape(bp, sp, D_MODEL)
    y_sample = _rmsnorm(xs, final_norm_g, NORM_EPS, F32, tm=bd).reshape(bd, td, D_MODEL)
    st = lambda n: jnp.stack(outs[n])
    return (y_prompt, y_sample, k_all.reshape(DEPTH, bp, sp, H_DA, dh2), v_all.reshape(DEPTH, bp, sp, H_DA, dh2),
            st("ks"), st("vs"), st("cp"), st("np"), st("mp"),
            st("cs"), st("ns"), st("ms"), st("cvp"), st("cvs"), st("hp"), st("hs"))
```

```python
import functools
import math

import jax
import jax.numpy as jnp
from jax import lax
from jax.experimental import pallas as pl
from jax.experimental.pallas import tpu as pltpu

D_MODEL = 4096
DEPTH = 2
PAGE_SIZE = 128
D_ATTN = D_MODEL // 2
D_MLSTM = D_MODEL // 4
D_RG = D_MODEL - D_ATTN - D_MLSTM
DH_DA = 128
H_DA = D_ATTN // (2 * DH_DA)
H_M = 4
DH_M = D_MLSTM // H_M
RG_BLOCKS = 8
RG_BW = D_RG // RG_BLOCKS
CONV_W = 4
RG_C = 8.0
D_FF = 4 * D_MODEL
ROPE_THETA = 10000.0
MLSTM_CHUNK = 128
NORM_EPS = 1e-6
SUBLN_EPS = 1e-5
NEG = -1e30
D_MAIN = 3 * D_ATTN + 4 * D_MLSTM
ATTN_QSCALE = DH_DA ** -0.5 * math.log2(math.e)

MXU_COLS = 256
LANES = 128
SUBLANES = 8
VMEM_LIMIT = 56 * 1024 * 1024

F32 = jnp.float32
BF16 = jnp.bfloat16


def _params(sem):
    return pltpu.CompilerParams(dimension_semantics=sem, vmem_limit_bytes=VMEM_LIMIT)


def _sigmoid(x):
    return 1.0 / (1.0 + jnp.exp(-x))


def _softplus(x):
    return jnp.maximum(x, 0.0) + jnp.log1p(jnp.exp(-jnp.abs(x)))


def _log_sigmoid(x):
    return -_softplus(-x)


def _gelu_tanh(x):
    c = math.sqrt(2.0 / math.pi)
    return 0.5 * x * (1.0 + jnp.tanh(c * (x + 0.044715 * (x * x * x))))


def _lambda(lq1, lk1, lq2, lk2, lam_init):
    a = jnp.exp(jnp.sum(lq1[...] * lk1[...], axis=-1, keepdims=True))
    b = jnp.exp(jnp.sum(lq2[...] * lk2[...], axis=-1, keepdims=True))
    return a - b + lam_init


def _rotate(x, cos, sin_signed):
    return x * cos + pltpu.roll(x, DH_DA // 2, axis=1) * sin_signed


def _rmsnorm_kernel(x_ref, g_ref, o_ref, *, eps):
    x = x_ref[...].astype(F32)
    ms = jnp.mean(x * x, axis=-1, keepdims=True)
    o_ref[...] = (x * lax.rsqrt(ms + eps) * g_ref[...]).astype(o_ref.dtype)


def _rmsnorm(x, g, eps, out_dtype, tm):
    m, d = x.shape
    return pl.pallas_call(
        functools.partial(_rmsnorm_kernel, eps=eps),
        grid=(m // tm,),
        in_specs=[pl.BlockSpec((tm, d), lambda i: (i, 0)),
                  pl.BlockSpec((1, d), lambda i: (0, 0))],
        out_specs=pl.BlockSpec((tm, d), lambda i: (i, 0)),
        out_shape=jax.ShapeDtypeStruct((m, d), out_dtype),
        compiler_params=_params(("parallel",)),
        name="rmsnorm",
    )(x, g.reshape(1, d).astype(F32))


def _mm_kernel(*refs, nk, seg, w_t, n_out, copy_groups, has_res, has_rope, rope_scale, act, tn,
               has_scale, norm_next):
    it = iter(refs)
    x_refs = [next(it) for _ in seg]
    w_ref = next(it)
    cos_ref = next(it) if has_rope else None
    sin_ref = next(it) if has_rope else None
    res_ref = next(it) if has_res else None
    ss_ref = next(it) if has_scale else None
    g_ref = next(it) if norm_next else None
    copy_in = [[next(it) for _ in range(depth)] for depth in copy_groups]
    o_refs = [next(it) for _ in range(n_out)]
    xg_ref = next(it) if norm_next else None
    ss_out = next(it) if norm_next else None
    copy_out = [next(it) for _ in copy_groups]

    for srcs, dst in zip(copy_in, copy_out):
        for d, src in enumerate(srcs):
            dst[d] = src[...]

    def emit_norm_inputs(x_new):
        xg_ref[...] = (x_new * g_ref[...]).astype(xg_ref.dtype)
        rows = jnp.broadcast_to(jnp.sum(x_new * x_new, axis=1, keepdims=True), ss_out.shape)
        j = pl.program_id(1)

        @pl.when(j == 0)
        def _():
            ss_out[...] = rows

        @pl.when(j > 0)
        def _():
            ss_out[...] += rows

    if nk > 1:
        o = o_refs[0]
        k = pl.program_id(2)
        chunks = [slice(c, c + MXU_COLS) for c in range(0, tn, MXU_COLS)]

        @pl.when(k == 0)
        def _():
            for sl in chunks:
                d = jnp.dot(x_refs[0][...], w_ref[:, sl], preferred_element_type=F32)
                o[:, sl] = res_ref[:, sl] + d if has_res else d

        @pl.when(k > 0)
        def _():
            for sl in chunks:
                o[:, sl] += jnp.dot(x_refs[0][...], w_ref[:, sl], preferred_element_type=F32)

        if norm_next:
            pl.when(k == nk - 1)(lambda: emit_norm_inputs(o[...]))
        return

    if w_t:
        part = lax.dot_general(x_refs[0][...], w_ref[...], (((1,), (1,)), ((), ())),
                               preferred_element_type=F32)
    elif len(seg) == 1:
        part = jnp.dot(x_refs[0][...].astype(BF16), w_ref[...], preferred_element_type=F32)
    else:
        part, off = None, 0
        for x_ref, width in zip(x_refs, seg):
            d = jnp.dot(x_ref[...], w_ref[off:off + width, :], preferred_element_type=F32)
            part = d if part is None else part + d
            off += width

    def store(sl, val):
        for o in o_refs:
            o[:, sl] = val.astype(o.dtype)

    acc = part
    if has_scale:
        acc = acc * lax.rsqrt(ss_ref[:, 0:1] * (1.0 / D_MODEL) + NORM_EPS)
    if has_rope:
        cos = cos_ref[...]
        sin = sin_ref[...]
        for g in range(tn // DH_DA):
            sl = slice(g * DH_DA, (g + 1) * DH_DA)
            store(sl, _rotate(acc[:, sl], cos, sin) * rope_scale)
        return
    if act == "relu2":
        r = jnp.maximum(acc, 0.0)
        acc = r * r
    if has_res:
        acc = res_ref[...] + acc
    store(slice(None), acc)
    if norm_next:
        emit_norm_inputs(acc)


def _matmul(xs, w, *, tm, tn, tk, out_dtypes, n=None, w_col0=0, w_t=False, residual=None, rope=None,
            rope_scale=1.0, act=None, row_ss=None, next_gain=None, stack=(), name="matmul"):
    xs = list(xs) if isinstance(xs, (list, tuple)) else [xs]
    seg = [x.shape[1] for x in xs]
    m = xs[0].shape[0]
    kdim = sum(seg)
    n = w.shape[0 if w_t else 1] - w_col0 if n is None else n
    nk = kdim // tk
    ni, nj = m // tm, n // tn
    assert len(xs) == 1 or (nk == 1 and not w_t)
    if nk > 1:
        assert (not w_t and rope is None and act is None and row_ss is None and not stack
                and tuple(out_dtypes) == (F32,) and xs[0].dtype == BF16)
    jb = w_col0 // tn

    if len(xs) == 1:
        in_specs = [pl.BlockSpec((tm, tk), lambda i, j, k: (i, k))]
    else:
        in_specs = [pl.BlockSpec((tm, s), lambda i, j, k: (i, 0)) for s in seg]
    if w_t:
        in_specs.append(pl.BlockSpec((tn, tk), lambda i, j, k: (j + jb, k)))
    else:
        in_specs.append(pl.BlockSpec((tk, tn), lambda i, j, k: (k, j + jb)))
    args = xs + [w]
    if rope is not None:
        nr = rope[0].shape[0] // tm
        spec = pl.BlockSpec((tm, DH_DA), lambda i, j, k: (i % nr, 0))
        in_specs += [spec, spec]
        args += [rope[0], rope[1]]
    if residual is not None:
        in_specs.append(pl.BlockSpec((tm, tn), lambda i, j, k: (i, j)))
        args.append(residual)
    if row_ss is not None:
        in_specs.append(pl.BlockSpec((tm, LANES), lambda i, j, k: (i, 0)))
        args.append(row_ss)
    if next_gain is not None:
        in_specs.append(pl.BlockSpec((1, tn), lambda i, j, k: (0, j)))
        args.append(next_gain.reshape(1, n).astype(F32))
    out_specs = [pl.BlockSpec((tm, tn), lambda i, j, k: (i, j)) for _ in out_dtypes]
    out_shapes = [jax.ShapeDtypeStruct((m, n), dt) for dt in out_dtypes]
    if next_gain is not None:
        out_specs += [pl.BlockSpec((tm, tn), lambda i, j, k: (i, j)),
                      pl.BlockSpec((tm, LANES), lambda i, j, k: (i, 0))]
        out_shapes += [jax.ShapeDtypeStruct((m, n), BF16), jax.ShapeDtypeStruct((m, LANES), F32)]
    slab = m // (ni * nj)
    for group in stack:
        width = group[0].shape[1]
        for arr in group:
            in_specs.append(pl.BlockSpec((slab, width), lambda i, j, k: (i * nj + j, 0)))
            args.append(arr)
        out_specs.append(pl.BlockSpec((len(group), slab, width), lambda i, j, k: (0, i * nj + j, 0)))
        out_shapes.append(jax.ShapeDtypeStruct((len(group), m, width), group[0].dtype))
    outs = pl.pallas_call(
        functools.partial(_mm_kernel, nk=nk, seg=seg, w_t=w_t, n_out=len(out_dtypes),
                          copy_groups=[len(g) for g in stack],
                          has_res=residual is not None, has_rope=rope is not None, rope_scale=rope_scale,
                          act=act, tn=tn, has_scale=row_ss is not None, norm_next=next_gain is not None),
        grid=(ni, nj, nk),
        in_specs=in_specs,
        out_specs=out_specs,
        out_shape=out_shapes,
        compiler_params=_params(("parallel", "arbitrary", "arbitrary")),
        name=name,
    )(*args)
    return outs[0] if len(outs) == 1 else outs


def _cast_mm_kernel(*refs, nk, has_res, act):
    it = iter(refs)
    x_ref = next(it)
    w_ref = next(it)
    res_ref = next(it) if has_res else None
    wb_ref = next(it)
    y_ref = next(it)
    k = pl.program_id(1)
    wb = w_ref[...].astype(BF16)
    wb_ref[...] = wb
    part = jnp.dot(x_ref[...].astype(BF16), wb, preferred_element_type=F32)

    @pl.when(k == 0)
    def _():
        y_ref[...] = part

    @pl.when(k > 0)
    def _():
        y_ref[...] += part

    if act is not None or has_res:
        @pl.when(k == nk - 1)
        def _():
            acc = y_ref[...]
            if act == "relu2":
                r = jnp.maximum(acc, 0.0)
                acc = r * r
            if has_res:
                acc = res_ref[...] + acc
            y_ref[...] = acc


def _cast_matmul(x, w_all, layer, n, *, tn, tk, residual=None, act=None, name="cast_matmul"):
    m, kdim = x.shape
    nk = kdim // tk
    in_specs = [pl.BlockSpec((m, tk), lambda j, k: (0, k)),
                pl.BlockSpec((None, tk, tn), lambda j, k: (layer, k, j))]
    args = [x, w_all]
    if residual is not None:
        in_specs.append(pl.BlockSpec((m, tn), lambda j, k: (0, j)))
        args.append(residual)
    wb, y = pl.pallas_call(
        functools.partial(_cast_mm_kernel, nk=nk, has_res=residual is not None, act=act),
        grid=(n // tn, nk),
        in_specs=in_specs,
        out_specs=[pl.BlockSpec((tk, tn), lambda j, k: (k, j)),
                   pl.BlockSpec((m, tn), lambda j, k: (0, j))],
        out_shape=[jax.ShapeDtypeStruct((kdim, n), BF16), jax.ShapeDtypeStruct((m, n), F32)],
        compiler_params=_params(("parallel", "arbitrary")),
        name=name,
    )(*args)
    return y, wb


def _nt_dot(x, wt):
    return lax.dot_general(x, wt, (((1,), (1,)), ((), ())), preferred_element_type=F32)


def _cast_mm_t_kernel(x_ref, w_ref, wb_ref, y_ref):
    wb = w_ref[...].astype(BF16)
    wb_ref[...] = wb
    y_ref[...] = _nt_dot(x_ref[...].astype(BF16), wb)


def _cast_matmul_t(x, wt_all, layer, n, *, tn, name):
    m, kdim = x.shape
    wb, y = pl.pallas_call(
        _cast_mm_t_kernel,
        grid=(n // tn,),
        in_specs=[pl.BlockSpec((m, kdim), lambda j: (0, 0)),
                  pl.BlockSpec((None, tn, kdim), lambda j: (layer, j, 0))],
        out_specs=[pl.BlockSpec((tn, kdim), lambda j: (j, 0)),
                   pl.BlockSpec((m, tn), lambda j: (0, j))],
        out_shape=[jax.ShapeDtypeStruct((n, kdim), BF16), jax.ShapeDtypeStruct((m, n), F32)],
        compiler_params=_params(("parallel",)),
        name=name,
    )(x, wt_all)
    return y, wb


def _cast_tail_kernel(x_ref, a_ref, b_ref, wr_ref, wg_ref, yr_ref, yg_ref):
    j = pl.program_id(0)
    xb = x_ref[...].astype(BF16)
    a = a_ref[...]
    tile = jnp.concatenate([a[SUBLANES:], b_ref[...]], axis=0).astype(BF16)
    wr_ref[...] = tile
    yr_ref[...] = _nt_dot(xb, tile)

    @pl.when(j == 0)
    def _():
        pad = jnp.zeros((LANES - SUBLANES, a.shape[1]), F32)
        g = jnp.concatenate([a[:SUBLANES], pad], axis=0).astype(BF16)
        wg_ref[...] = g
        yg_ref[...] = _nt_dot(xb, g)


def _cast_tail(x, wt_all, layer, *, tn, name):
    assert 2 * H_M == SUBLANES
    m, kdim = x.shape
    n = 2 * D_RG
    blk0 = D_MAIN // tn
    return pl.pallas_call(
        _cast_tail_kernel,
        grid=(n // tn,),
        in_specs=[pl.BlockSpec((m, kdim), lambda j: (0, 0)),
                  pl.BlockSpec((None, tn, kdim), lambda j: (layer, blk0 + j, 0)),
                  pl.BlockSpec((None, SUBLANES, kdim), lambda j: (layer, (blk0 + j + 1) * (tn // SUBLANES), 0))],
        out_specs=[pl.BlockSpec((tn, kdim), lambda j: (j, 0)),
                   pl.BlockSpec((LANES, kdim), lambda j: (0, 0)),
                   pl.BlockSpec((m, tn), lambda j: (0, j)),
                   pl.BlockSpec((m, LANES), lambda j: (0, 0))],
        out_shape=[jax.ShapeDtypeStruct((n, kdim), BF16), jax.ShapeDtypeStruct((LANES, kdim), BF16),
                   jax.ShapeDtypeStruct((m, n), F32), jax.ShapeDtypeStruct((m, LANES), F32)],
        compiler_params=_params(("arbitrary",)),
        name=name,
    )(x, wt_all, wt_all)


def _rope_rows_kernel(x_ref, cos_ref, sin_ref, o_ref, *, n_q):
    cos = cos_ref[...]
    sin = sin_ref[...]
    for g in range(x_ref.shape[1] // DH_DA):
        sl = slice(g * DH_DA, (g + 1) * DH_DA)
        y = _rotate(x_ref[:, sl], cos, sin)
        o_ref[:, sl] = y * ATTN_QSCALE if g < n_q else y


def _rope_rows(x, cos, sin):
    m, n = x.shape
    full = lambda c: pl.BlockSpec((m, c), lambda i: (0, 0))
    return pl.pallas_call(
        functools.partial(_rope_rows_kernel, n_q=D_ATTN // DH_DA),
        grid=(1,),
        in_specs=[full(n), full(DH_DA), full(DH_DA)],
        out_specs=full(n),
        out_shape=jax.ShapeDtypeStruct((m, n), F32),
        compiler_params=_params(("arbitrary",)),
        name="rope_sample",
    )(x, cos, sin)


def _attn_prompt_kernel(lq1, lk1, lq2, lk2, g_ref, q_ref, k_ref, v_ref, o_ref, *, t, nq, hp, lam_init):
    qi = pl.program_id(2)
    dh2 = 2 * DH_DA
    lam = _lambda(lq1, lk1, lq2, lk2, lam_init)
    tri = (lax.broadcasted_iota(jnp.int32, (t, t), 1) <= lax.broadcasted_iota(jnp.int32, (t, t), 0))

    def head(n_below, h):
        spans = ([(0, n_below * t, False)] if n_below else []) + [(n_below * t, (n_below + 1) * t, True)]
        probs, inv = [], []
        for c in range(2):
            sl = slice(h * dh2 + c * DH_DA, h * dh2 + (c + 1) * DH_DA)
            ss = []
            for lo, hi, diagonal in spans:
                s = lax.dot_general(q_ref[:, sl], k_ref[lo:hi, sl], (((1,), (1,)), ((), ())),
                                    preferred_element_type=F32)
                ss.append(jnp.where(tri, s, NEG) if diagonal else s)
            m = functools.reduce(jnp.maximum, [jnp.max(s, axis=1, keepdims=True) for s in ss])
            ps = [jnp.exp2(s - m) for s in ss]
            probs.append(ps)
            inv.append(1.0 / functools.reduce(jnp.add, [jnp.sum(p, axis=1, keepdims=True) for p in ps]))
        r1 = inv[0]
        r2 = lam * inv[1]
        hs = slice(h * dh2, (h + 1) * dh2)
        out = None
        for i, (lo, hi, _) in enumerate(spans):
            a = (probs[0][i] * r1 - probs[1][i] * r2).astype(BF16)
            d = jnp.dot(a, v_ref[lo:hi, hs], preferred_element_type=F32)
            out = d if out is None else out + d
        ms = jnp.mean(out * out, axis=-1, keepdims=True)
        y = out * lax.rsqrt(ms + SUBLN_EPS) * g_ref[...]
        o_ref[:, hs] = (y * (1.0 - lam_init)).astype(o_ref.dtype)

    def block(n_below):
        for h in range(hp):
            head(n_below, h)

    for n_below in range(nq):
        pl.when(qi == n_below)(functools.partial(block, n_below))


def _attn_prompt(q, k, v, lam_rows, subln_g, lam_init, batch, seq, t=512, hp=2):
    nq = seq // t
    dh2 = 2 * DH_DA
    row_spec = pl.BlockSpec((1, DH_DA), lambda b, h, i: (0, 0))
    return pl.pallas_call(
        functools.partial(_attn_prompt_kernel, t=t, nq=nq, hp=hp, lam_init=lam_init),
        grid=(batch, H_DA // hp, nq),
        in_specs=[row_spec, row_spec, row_spec, row_spec,
                  pl.BlockSpec((1, dh2), lambda b, h, i: (0, 0)),
                  pl.BlockSpec((t, hp * dh2), lambda b, h, i: (b * nq + i, h)),
                  pl.BlockSpec((seq, hp * dh2), lambda b, h, i: (b, h)),
                  pl.BlockSpec((seq, hp * dh2), lambda b, h, i: (b, h))],
        out_specs=pl.BlockSpec((t, hp * dh2), lambda b, h, i: (b * nq + i, h)),
        out_shape=jax.ShapeDtypeStruct((batch * seq, D_ATTN), BF16),
        compiler_params=_params(("parallel", "parallel", "arbitrary")),
        name="attn_prompt",
    )(*lam_rows, subln_g.reshape(1, dh2), q, k, v)


def _attn_sample_kernel(*refs, lam_init, n_steps, group):
    pt_ref, lq1, lk1, lq2, lk2, g_ref, q_ref, kn_ref, vn_ref = refs[:9]
    k_refs = refs[9:9 + group]
    v_refs = refs[9 + group:9 + 2 * group]
    o_ref, m_sc, l_sc, acc_sc = refs[9 + 2 * group:]
    del pt_ref
    step = pl.program_id(1)
    rows = PAGE_SIZE * H_DA

    @pl.when(step == 0)
    def _():
        m_sc[...] = jnp.full(m_sc.shape, NEG, F32)
        l_sc[...] = jnp.zeros(l_sc.shape, F32)
        acc_sc[...] = jnp.zeros(acc_sc.shape, F32)

    q = q_ref[...]
    zero = jnp.zeros((H_DA, DH_DA), F32)
    qbd = jnp.concatenate([jnp.concatenate([q[:, :DH_DA], zero], axis=1),
                           jnp.concatenate([zero, q[:, DH_DA:]], axis=1)], axis=0).astype(BF16)
    own_head = (lax.broadcasted_iota(jnp.int32, (2 * H_DA, rows), 0) % H_DA
                == lax.broadcasted_iota(jnp.int32, (2 * H_DA, rows), 1) % H_DA)
    scores = []
    for g in range(group):
        k2 = k_refs[g][...].reshape(rows, 2 * DH_DA).astype(BF16)
        s = lax.dot_general(qbd, k2, (((1,), (1,)), ((), ())), preferred_element_type=F32)
        scores.append(jnp.where(own_head, s, NEG))
    m_prev = m_sc[...]
    m_new = m_prev
    for s in scores:
        m_new = jnp.maximum(m_new, jnp.max(s, axis=1, keepdims=True))
    alpha = jnp.exp2(m_prev - m_new)
    l_new = alpha * l_sc[...]
    acc = alpha * acc_sc[...]
    for g in range(group):
        p = jnp.exp2(scores[g] - m_new)
        l_new = l_new + jnp.sum(p, axis=1, keepdims=True)
        v2 = v_refs[g][...].reshape(rows, 2 * DH_DA).astype(BF16)
        acc = acc + jnp.dot(p.astype(BF16), v2, preferred_element_type=F32)
    m_sc[...] = m_new
    l_sc[...] = l_new
    acc_sc[...] = acc

    @pl.when(step == n_steps - 1)
    def _():
        kn = kn_ref[...]
        vn = vn_ref[...]
        pn = kn * q
        sn = jnp.concatenate([jnp.sum(pn[:, :DH_DA], axis=-1, keepdims=True),
                              jnp.sum(pn[:, DH_DA:], axis=-1, keepdims=True)], axis=0)
        m_fin = jnp.maximum(m_new, sn)
        pe = jnp.exp2(sn - m_fin)
        a_fin = jnp.exp2(m_new - m_fin)
        l_fin = a_fin * l_new + pe
        outs = (a_fin * acc + pe * jnp.concatenate([vn, vn], axis=0)) / l_fin
        lam = _lambda(lq1, lk1, lq2, lk2, lam_init)
        out = outs[0:H_DA] - lam * outs[H_DA:]
        ms = jnp.mean(out * out, axis=-1, keepdims=True)
        y = out * lax.rsqrt(ms + SUBLN_EPS) * g_ref[...]
        o_ref[...] = (y * (1.0 - lam_init)).astype(o_ref.dtype)


def _attn_sample(page_table, q, k_new, v_new, cache_k, cache_v, layer, lam_rows, subln_g, lam_init, group=8):
    bd, n_pages = page_table.shape
    n_steps = n_pages // group
    dh2 = 2 * DH_DA
    row_spec = pl.BlockSpec((1, DH_DA), lambda b, p, pt: (0, 0))
    tok_spec = pl.BlockSpec((None, H_DA, dh2), lambda b, p, pt: (b, 0, 0))

    def page_spec(g):
        return pl.BlockSpec((None, None, PAGE_SIZE, H_DA, dh2),
                            lambda b, p, pt: (layer, pt[b, p * group + g], 0, 0, 0))

    pages = [page_spec(g) for g in range(group)]
    grid_spec = pltpu.PrefetchScalarGridSpec(
        num_scalar_prefetch=1,
        grid=(bd, n_steps),
        in_specs=[row_spec, row_spec, row_spec, row_spec,
                  pl.BlockSpec((1, dh2), lambda b, p, pt: (0, 0)),
                  tok_spec, tok_spec, tok_spec] + pages + pages,
        out_specs=tok_spec,
        scratch_shapes=[pltpu.VMEM((2 * H_DA, 1), F32), pltpu.VMEM((2 * H_DA, 1), F32),
                        pltpu.VMEM((2 * H_DA, dh2), F32)],
    )
    return pl.pallas_call(
        functools.partial(_attn_sample_kernel, lam_init=lam_init, n_steps=n_steps, group=group),
        grid_spec=grid_spec,
        out_shape=jax.ShapeDtypeStruct((bd, H_DA, dh2), F32),
        compiler_params=_params(("parallel", "arbitrary")),
        name="attn_sample",
    )(page_table, *lam_rows, subln_g.reshape(1, dh2), q, k_new, v_new,
      *([cache_k] * group), *([cache_v] * group))


def _mlstm_out(h, g, om):
    mu = jnp.mean(h, axis=-1, keepdims=True)
    hc = h - mu
    var = jnp.mean(hc * hc, axis=-1, keepdims=True)
    return hc * lax.rsqrt(var + NORM_EPS) * g * _sigmoid(om)


def _mlstm_prompt_kernel(big_ref, bfg_ref, q_ref, k_ref, v_ref, om_ref, gc_ref, gr_ref, ng_ref,
                         hm_ref, c_out, n_out, m_out, c_sc, n_sc, m_sc, *, n_chunks):
    ci = pl.program_id(1)
    L = MLSTM_CHUNK

    @pl.when(ci == 0)
    def _():
        c_sc[...] = jnp.zeros(c_sc.shape, F32)
        n_sc[...] = jnp.zeros(n_sc.shape, F32)
        m_sc[...] = jnp.zeros(m_sc.shape, F32)

    ri = lax.broadcasted_iota(jnp.int32, (L, L), 0)
    cj = lax.broadcasted_iota(jnp.int32, (L, L), 1)
    lower = ri >= cj
    gc = gc_ref[...]
    gr = gr_ref[...]
    for h in range(H_M):
        hs = slice(h * DH_M, (h + 1) * DH_M)
        ig_c = gc[:, h:h + 1] + big_ref[h]
        lf_c = _log_sigmoid(gc[:, H_M + h:H_M + h + 1] + bfg_ref[h])
        ig_r = gr[h:h + 1, :] + big_ref[h]
        lf_r = _log_sigmoid(gr[H_M + h:H_M + h + 1, :] + bfg_ref[h])
        b_col = jnp.sum(jnp.where(lower, lf_r, 0.0), axis=1, keepdims=True)
        b_row = jnp.sum(jnp.where(cj >= ri, lf_c, 0.0), axis=0, keepdims=True)
        dm = jnp.where(lower, b_col - b_row + ig_r, -jnp.inf)
        m_prev = m_sc[h:h + 1, 0:1]
        inter = b_col + m_prev
        m_t = jnp.maximum(inter, jnp.max(dm, axis=1, keepdims=True))
        w_intra = jnp.exp(dm - m_t)
        w_inter = jnp.exp(inter - m_t)
        q = q_ref[:, hs]
        ks = k_ref[:, hs] * (DH_M ** -0.5)
        v = v_ref[:, hs]
        c_prev = c_sc[h]
        n_prev = n_sc[h:h + 1, :]
        sc = w_intra * lax.dot_general(q, ks, (((1,), (1,)), ((), ())), preferred_element_type=F32)
        num = (jnp.dot(sc.astype(BF16), v, preferred_element_type=F32)
               + w_inter * lax.dot_general(q, c_prev.astype(BF16), (((1,), (1,)), ((), ())),
                                           preferred_element_type=F32))
        den = (jnp.sum(sc, axis=1, keepdims=True)
               + w_inter * jnp.sum(q.astype(F32) * n_prev, axis=1, keepdims=True))
        hh = num / jnp.maximum(jnp.abs(den), jnp.exp(-m_t))
        m_new = m_t[L - 1:L, :]
        decay = w_inter[L - 1:L, :]
        wk = jnp.exp(b_col[L - 1:L, :] - b_col + ig_c - m_new)
        wv = (wk * v.astype(F32)).astype(BF16)
        c_sc[h] = decay * c_prev + lax.dot_general(wv, ks, (((0,), (0,)), ((), ())),
                                                   preferred_element_type=F32)
        n_sc[h:h + 1, :] = decay * n_prev + jnp.sum(wk * ks.astype(F32), axis=0, keepdims=True)
        m_sc[h:h + 1, :] = jnp.broadcast_to(m_new, (1, LANES))
        hm_ref[:, hs] = _mlstm_out(hh, ng_ref[:, hs], om_ref[:, hs].astype(F32)).astype(hm_ref.dtype)

    @pl.when(ci == n_chunks - 1)
    def _():
        c_out[...] = c_sc[...]
        n_out[...] = n_sc[0:H_M, :]
        m_out[...] = m_sc[0:H_M, :]


def _mlstm_prompt(um, gates, gates_t, b_ig, b_fg, norm_g, batch, seq):
    L = MLSTM_CHUNK
    nc = seq // L
    smem = pl.BlockSpec(memory_space=pltpu.SMEM)

    def col(cb):
        return pl.BlockSpec((L, D_MLSTM), lambda b, c: (b * nc + c, cb))

    return pl.pallas_call(
        functools.partial(_mlstm_prompt_kernel, n_chunks=nc),
        grid=(batch, nc),
        in_specs=[smem, smem, col(0), col(1), col(2), col(3),
                  pl.BlockSpec((L, LANES), lambda b, c: (b * nc + c, 0)),
                  pl.BlockSpec((None, SUBLANES, L), lambda b, c: (b, 0, c)),
                  pl.BlockSpec((1, D_MLSTM), lambda b, c: (0, 0))],
        out_specs=[pl.BlockSpec((L, D_MLSTM), lambda b, c: (b * nc + c, 0)),
                   pl.BlockSpec((None, H_M, DH_M, DH_M), lambda b, c: (b, 0, 0, 0)),
                   pl.BlockSpec((None, H_M, DH_M), lambda b, c: (b, 0, 0)),
                   pl.BlockSpec((None, H_M, LANES), lambda b, c: (b, 0, 0))],
        out_shape=[jax.ShapeDtypeStruct((batch * seq, D_MLSTM), BF16),
                   jax.ShapeDtypeStruct((batch, H_M, DH_M, DH_M), F32),
                   jax.ShapeDtypeStruct((batch, H_M, DH_M), F32),
                   jax.ShapeDtypeStruct((batch, H_M, LANES), F32)],
        scratch_shapes=[pltpu.VMEM((H_M, DH_M, DH_M), F32), pltpu.VMEM((SUBLANES, DH_M), F32),
                        pltpu.VMEM((SUBLANES, LANES), F32)],
        compiler_params=_params(("parallel", "arbitrary")),
        name="mlstm_prompt",
    )(b_ig, b_fg, um, um, um, um, gates, gates_t, norm_g.reshape(1, D_MLSTM))


def _mlstm_sample_kernel(big_ref, bfg_ref, u_ref, vcol_ref, g_ref, ng_ref, c_ref, n_ref, m_ref,
                         hm_ref, c_out, n_out, m_out):
    u = u_ref[...]
    g = g_ref[...]
    for h in range(H_M):
        def part(i):
            return u[:, i * D_MLSTM + h * DH_M:i * D_MLSTM + (h + 1) * DH_M]
        q, k, v, om = part(0), part(1), part(2), part(3)
        ks = k * (DH_M ** -0.5)
        ig = g[:, h:h + 1] + big_ref[h]
        lf = _log_sigmoid(g[:, H_M + h:H_M + h + 1] + bfg_ref[h])
        m_prev = m_ref[:, h:h + 1]
        inter = lf + m_prev
        m_t = jnp.maximum(inter, ig)
        w_intra = jnp.exp(ig - m_t)
        w_inter = jnp.exp(inter - m_t)
        c_prev = c_ref[h]
        n_prev = n_ref[h:h + 1, :]
        q8 = jnp.broadcast_to(q, (2 * SUBLANES, DH_M)).astype(BF16)
        cq = lax.dot_general(q8, c_prev.astype(BF16), (((1,), (1,)), ((), ())),
                             preferred_element_type=F32)[0:1, :]
        sc = w_intra * jnp.sum(q * ks, axis=-1, keepdims=True)
        num = sc * v + w_inter * cq
        den = sc + w_inter * jnp.sum(n_prev * q, axis=-1, keepdims=True)
        hh = num / jnp.maximum(jnp.abs(den), jnp.exp(-m_t))
        c_out[h] = w_inter * c_prev + (w_intra * vcol_ref[h]) * ks
        n_out[h:h + 1, :] = w_inter * n_prev + w_intra * ks
        m_out[h:h + 1, :] = jnp.broadcast_to(m_t, (1, LANES))
        hs = slice(h * DH_M, (h + 1) * DH_M)
        hm_ref[:, hs] = _mlstm_out(hh, ng_ref[:, hs], om)


def _mlstm_sample(um, gates, b_ig, b_fg, norm_g, c_state, n_state, m_state):
    bd = um.shape[0]
    smem = pl.BlockSpec(memory_space=pltpu.SMEM)
    u4 = um.reshape(bd, 1, 4 * D_MLSTM)
    vcol = um[:, 2 * D_MLSTM:3 * D_MLSTM].reshape(bd, H_M, DH_M, 1)
    return pl.pallas_call(
        _mlstm_sample_kernel,
        grid=(bd,),
        in_specs=[smem, smem,
                  pl.BlockSpec((None, 1, 4 * D_MLSTM), lambda b: (b, 0, 0)),
                  pl.BlockSpec((None, H_M, DH_M, 1), lambda b: (b, 0, 0, 0)),
                  pl.BlockSpec((None, 1, LANES), lambda b: (b, 0, 0)),
                  pl.BlockSpec((1, D_MLSTM), lambda b: (0, 0)),
                  pl.BlockSpec((None, H_M, DH_M, DH_M), lambda b: (b, 0, 0, 0)),
                  pl.BlockSpec((None, H_M, DH_M), lambda b: (b, 0, 0)),
                  pl.BlockSpec((None, 1, H_M), lambda b: (b, 0, 0))],
        out_specs=[pl.BlockSpec((None, 1, D_MLSTM), lambda b: (b, 0, 0)),
                   pl.BlockSpec((None, H_M, DH_M, DH_M), lambda b: (b, 0, 0, 0)),
                   pl.BlockSpec((None, H_M, DH_M), lambda b: (b, 0, 0)),
                   pl.BlockSpec((None, H_M, LANES), lambda b: (b, 0, 0))],
        out_shape=[jax.ShapeDtypeStruct((bd, 1, D_MLSTM), F32),
                   jax.ShapeDtypeStruct((bd, H_M, DH_M, DH_M), F32),
                   jax.ShapeDtypeStruct((bd, H_M, DH_M), F32),
                   jax.ShapeDtypeStruct((bd, H_M, LANES), F32)],
        compiler_params=_params(("parallel",)),
        name="mlstm_sample",
    )(b_ig, b_fg, u4, vcol, gates.reshape(bd, 1, LANES), norm_g.reshape(1, D_MLSTM),
      c_state, n_state, m_state.reshape(bd, 1, H_M))


def _rglru_coeffs(xc, wra_ref, bra_ref, wrx_ref, brx_ref, lam_ref):
    rs, is_ = [], []
    for n in range(RG_BLOCKS):
        xb = xc[:, n * RG_BW:(n + 1) * RG_BW].astype(BF16)
        rs.append(jnp.dot(xb, wra_ref[n], preferred_element_type=F32))
        is_.append(jnp.dot(xb, wrx_ref[n], preferred_element_type=F32))
    r = _sigmoid(jnp.concatenate(rs, axis=-1) + bra_ref[...])
    i = _sigmoid(jnp.concatenate(is_, axis=-1) + brx_ref[...])
    log_a = -RG_C * r * _softplus(-lam_ref[...])
    a = jnp.exp(log_a)
    mult = jnp.sqrt(1.0 - a * a)
    return a, mult * i * xc


def _rglru_prompt_kernel(x_ref, gate_ref, cw_ref, cb_ref, wra_ref, bra_ref, wrx_ref, brx_ref, lam_ref,
                         y_ref, conv_out, h_out, xbuf, a_sc, b_sc, h_sc, *, tt, n_tiles):
    ti = pl.program_id(1)
    pad = SUBLANES

    @pl.when(ti == 0)
    def _():
        xbuf[0:pad, :] = jnp.zeros((pad, D_RG), F32)
        h_sc[...] = jnp.zeros(h_sc.shape, F32)

    xbuf[pad:pad + tt, :] = x_ref[...].astype(F32)
    xc = cb_ref[...] + sum(xbuf[pad - (CONV_W - 1) + j:pad - (CONV_W - 1) + j + tt, :] * cw_ref[j:j + 1, :]
                           for j in range(CONV_W))
    a, b = _rglru_coeffs(xc, wra_ref, bra_ref, wrx_ref, brx_ref, lam_ref)
    a_sc[...] = a
    b_sc[...] = b

    def step(t, h):
        h = a_sc[pl.ds(t, 1), :] * h + b_sc[pl.ds(t, 1), :]
        b_sc[pl.ds(t, 1), :] = h
        return h

    h_last = lax.fori_loop(0, tt, step, h_sc[...], unroll=8)
    h_sc[...] = h_last
    y_ref[...] = (b_sc[...] * _gelu_tanh(gate_ref[...].astype(F32))).astype(y_ref.dtype)
    tail = xbuf[tt:tt + pad, :]
    xbuf[0:pad, :] = tail

    @pl.when(ti == n_tiles - 1)
    def _():
        conv_out[...] = tail[pad - (CONV_W - 1):, :]
        h_out[...] = h_last


def _rglru_weights(conv_w, conv_b, w_ra, b_ra, w_rx, b_rx, lam):
    row = lambda z: z.reshape(1, D_RG).astype(F32)
    return (conv_w.astype(F32), row(conv_b), w_ra.astype(BF16), row(b_ra), w_rx.astype(BF16), row(b_rx), row(lam))


def _rglru_weight_specs():
    zeros2 = (lambda *a: (0, 0))
    zeros3 = (lambda *a: (0, 0, 0))
    return [pl.BlockSpec((CONV_W, D_RG), zeros2), pl.BlockSpec((1, D_RG), zeros2),
            pl.BlockSpec((RG_BLOCKS, RG_BW, RG_BW), zeros3), pl.BlockSpec((1, D_RG), zeros2),
            pl.BlockSpec((RG_BLOCKS, RG_BW, RG_BW), zeros3), pl.BlockSpec((1, D_RG), zeros2),
            pl.BlockSpec((1, D_RG), zeros2)]


def _rglru_prompt(urg, weights, batch, seq, tt=256):
    nt = seq // tt
    return pl.pallas_call(
        functools.partial(_rglru_prompt_kernel, tt=tt, n_tiles=nt),
        grid=(batch, nt),
        in_specs=[pl.BlockSpec((tt, D_RG), lambda b, t: (b * nt + t, 0)),
                  pl.BlockSpec((tt, D_RG), lambda b, t: (b * nt + t, 1))] + _rglru_weight_specs(),
        out_specs=[pl.BlockSpec((tt, D_RG), lambda b, t: (b * nt + t, 0)),
                   pl.BlockSpec((None, CONV_W - 1, D_RG), lambda b, t: (b, 0, 0)),
                   pl.BlockSpec((None, 1, D_RG), lambda b, t: (b, 0, 0))],
        out_shape=[jax.ShapeDtypeStruct((batch * seq, D_RG), BF16),
                   jax.ShapeDtypeStruct((batch, CONV_W - 1, D_RG), F32),
                   jax.ShapeDtypeStruct((batch, 1, D_RG), F32)],
        scratch_shapes=[pltpu.VMEM((tt + SUBLANES, D_RG), F32), pltpu.VMEM((tt, D_RG), F32),
                        pltpu.VMEM((tt, D_RG), F32), pltpu.VMEM((1, D_RG), F32)],
        compiler_params=_params(("parallel", "arbitrary")),
        name="rglru_prompt",
    )(urg, urg, *weights)


def _rglru_sample_kernel(x_ref, gate_ref, p0_ref, p1_ref, p2_ref, h0_ref, cw_ref, cb_ref, wra_ref, bra_ref,
                         wrx_ref, brx_ref, lam_ref, y_ref, h_out):
    x = x_ref[...]
    xc = cb_ref[...] + (p0_ref[...] * cw_ref[0:1, :] + p1_ref[...] * cw_ref[1:2, :]
                        + p2_ref[...] * cw_ref[2:3, :] + x * cw_ref[3:4, :])
    a, b = _rglru_coeffs(xc, wra_ref, bra_ref, wrx_ref, brx_ref, lam_ref)
    h = a * h0_ref[...] + b
    h_out[...] = h
    y_ref[...] = h * _gelu_tanh(gate_ref[...])


def _rglru_sample(x, gate, conv_prev, h0, weights):
    bd = x.shape[0]
    full = pl.BlockSpec((bd, D_RG), lambda i: (0, 0))
    return pl.pallas_call(
        _rglru_sample_kernel,
        grid=(1,),
        in_specs=[full] * 6 + _rglru_weight_specs(),
        out_specs=[full, full],
        out_shape=[jax.ShapeDtypeStruct((bd, D_RG), F32)] * 2,
        compiler_params=_params(("arbitrary",)),
        name="rglru_sample",
    )(x, gate, conv_prev[:, 0], conv_prev[:, 1], conv_prev[:, 2], h0, *weights)


def _rope_tables(pos):
    inv = 1.0 / (ROPE_THETA ** (jnp.arange(0, DH_DA, 2, dtype=F32) / DH_DA))
    ang = pos.astype(F32)[:, None] * inv[None, :]
    ang = jnp.concatenate([ang, ang], axis=-1)
    sign = jnp.concatenate([-jnp.ones((DH_DA // 2,), F32), jnp.ones((DH_DA // 2,), F32)])
    return jnp.cos(ang), jnp.sin(ang) * sign[None, :]


def kernel(x_prompt, x_sample, cache_k, cache_v, page_table, state_mlstm_c, state_mlstm_n, state_mlstm_m, state_conv, state_rglru_h, norm_mix_g, w_in, w_out, lam_q1, lam_k1, lam_q2, lam_k2, attn_subln_g, b_ig, b_fg, mlstm_norm_g, conv_w, conv_b, w_ra, b_ra, w_rx, b_rx, rg_lambda, norm_mlp_g, w_up, w_down, final_norm_g):
    bp, sp, _ = x_prompt.shape
    bd, td, _ = x_sample.shape
    assert td == 1
    past = page_table.shape[1] * PAGE_SIZE
    mp = bp * sp
    dh2 = 2 * DH_DA
    cos_p, sin_p = _rope_tables(jnp.arange(sp))
    cos_s, sin_s = _rope_tables(jnp.full((bd,), past))

    TM = 1024
    TNC = 512

    xp = x_prompt.reshape(mp, D_MODEL)
    xs = x_sample.reshape(bd, D_MODEL)
    w_in_t = jnp.swapaxes(w_in, 1, 2)
    k_layers, v_layers = [], []
    outs = {n: [] for n in ("ks", "vs", "cp", "np", "mp", "cs", "ns", "ms", "cvp", "cvs", "hp", "hs")}
    for l in range(DEPTH):
        lam_init = 0.8 - 0.6 * math.exp(-0.3 * l)
        lam_rows = [z[l].reshape(1, DH_DA).astype(F32) for z in (lam_q1, lam_k1, lam_q2, lam_k2)]
        rg_w = _rglru_weights(conv_w[l], conv_b[l], w_ra[l], b_ra[l], w_rx[l], b_rx[l], rg_lambda[l])

        hs_n = _rmsnorm(xs, norm_mix_g[l], NORM_EPS, F32, tm=bd)
        u_s, w_in_b = _cast_matmul_t(hs_n, w_in_t, l, D_MAIN, tn=TNC, name="s_proj_in")
        wr, wg, urg_s, gates_s = _cast_tail(hs_n, w_in_t, l, tn=TNC, name="s_proj_tail")
        qk_s = _rope_rows(u_s[:, :2 * D_ATTN], cos_s, sin_s)
        q_s, k_s = qk_s[:, :D_ATTN], qk_s[:, D_ATTN:]
        v_s = u_s[:, 2 * D_ATTN:3 * D_ATTN]
        um_s = u_s[:, 3 * D_ATTN:]

        if l == 0:
            hn, ss = _rmsnorm(xp, norm_mix_g[l], NORM_EPS, BF16, tm=512), None
        mm = functools.partial(_matmul, hn, tm=TM, tk=D_MODEL, w_t=True, row_ss=ss)
        k32, k16 = mm(w_in_b, n=D_ATTN, w_col0=D_ATTN, tn=512, out_dtypes=(F32, BF16), rope=(cos_p, sin_p),
                      name="proj_k")
        v32, v16 = mm(w_in_b, n=D_ATTN, w_col0=2 * D_ATTN, tn=512, out_dtypes=(F32, BF16), name="proj_v")
        k_layers.append(k32)
        v_layers.append(v32)
        proj_q = functools.partial(mm, w_in_b, n=D_ATTN, w_col0=0, out_dtypes=(BF16,), rope=(cos_p, sin_p),
                                   rope_scale=ATTN_QSCALE, name="proj_q")
        proj_r = functools.partial(mm, wr, out_dtypes=(BF16,), name="proj_r")
        if l + 1 < DEPTH:
            q = proj_q(tn=1024)
            urg = proj_r(tn=1024)
        else:
            q, k_all = proj_q(tn=512, stack=(k_layers,))
            urg, v_all = proj_r(tn=512, stack=(v_layers,))
        um = mm(w_in_b, n=4 * D_MLSTM, w_col0=3 * D_ATTN, tn=1024, out_dtypes=(BF16,), name="proj_m")
        gates = mm(wg, tn=LANES, out_dtypes=(F32,), name="proj_g")
        gates_t = gates[:, :SUBLANES].reshape(bp, sp, SUBLANES).transpose(0, 2, 1)
        att = _attn_prompt(q, k16, v16, lam_rows, attn_subln_g[l], lam_init, bp, sp)
        hm, c_p, n_p, m_p = _mlstm_prompt(um, gates, gates_t, b_ig[l], b_fg[l], mlstm_norm_g[l], bp, sp)
        hr, cv_p, h_p = _rglru_prompt(urg, rg_w, bp, sp)

        att_s = _attn_sample(page_table, q_s.reshape(bd, H_DA, dh2), k_s.reshape(bd, H_DA, dh2),
                             v_s.reshape(bd, H_DA, dh2), cache_k, cache_v, l, lam_rows,
                             attn_subln_g[l], lam_init)
        hm_s, c_s, n_s, m_s = _mlstm_sample(um_s, gates_s, b_ig[l], b_fg[l], mlstm_norm_g[l],
                                            state_mlstm_c[l], state_mlstm_n[l], state_mlstm_m[l])
        x_rg = urg_s[:, :D_RG]
        hr_s, h_s = _rglru_sample(x_rg, urg_s[:, D_RG:], state_conv[l], state_rglru_h[l], rg_w)
        cat_s = jnp.concatenate([att_s.reshape(bd, D_ATTN), hm_s.reshape(bd, D_MLSTM), hr_s], axis=1)
        x1s, wo = _cast_matmul(cat_s, w_out, l, D_MODEL, tn=TNC, tk=D_MODEL, residual=xs, name="s_proj_out")
        hs_n2 = _rmsnorm(x1s, norm_mlp_g[l], NORM_EPS, F32, tm=bd)
        act_s, wu = _cast_matmul(hs_n2, w_up, l, D_FF, tn=TNC, tk=D_MODEL, act="relu2", name="s_mlp_up")
        xs, wd = _cast_matmul(act_s, w_down, l, D_MODEL, tn=TNC, tk=D_MODEL, residual=x1s, name="s_mlp_down")

        x1, xg1, ss1 = _matmul([att, hm, hr], wo, tm=TM, tn=512, tk=D_MODEL, out_dtypes=(F32,), residual=xp,
                               next_gain=norm_mlp_g[l], name="proj_out")
        act = _matmul(xg1, wu, tm=TM, tn=1024, tk=D_MODEL, out_dtypes=(BF16,), act="relu2", row_ss=ss1,
                      name="mlp_up")
        down = functools.partial(_matmul, act, wd, tm=TM, tn=1024, tk=D_MODEL, out_dtypes=(F32,), residual=x1,
                                 name="mlp_down")
        if l + 1 < DEPTH:
            xp, hn, ss = down(next_gain=norm_mix_g[l + 1])
        else:
            xp = down()

        outs["ks"].append(k_s.reshape(bd, td, H_DA, dh2))
        outs["vs"].append(v_s.reshape(bd, td, H_DA, dh2))
        outs["cp"].append(c_p)
        outs["np"].append(n_p)
        outs["mp"].append(m_p[:, :, 0])
        outs["cs"].append(c_s)
        outs["ns"].append(n_s)
        outs["ms"].append(m_s[:, :, 0])
        outs["cvp"].append(cv_p)
        outs["cvs"].append(jnp.concatenate([state_conv[l][:, 1:], x_rg[:, None, :]], axis=1))
        outs["hp"].append(h_p.reshape(bp, D_RG))
        outs["hs"].append(h_s)

    y_prompt = _rmsnorm(xp, final_norm_g, NORM_EPS, F32, tm=512).reshape(bp, sp, D_MODEL)
    y_sample = _rmsnorm(xs, final_norm_g, NORM_EPS, F32, tm=bd).reshape(bd, td, D_MODEL)
    st = lambda n: jnp.stack(outs[n])
    return (y_prompt, y_sample, k_all.reshape(DEPTH, bp, sp, H_DA, dh2), v_all.reshape(DEPTH, bp, sp, H_DA, dh2),
            st("ks"), st("vs"), st("cp"), st("np"), st("mp"),
            st("cs"), st("ns"), st("ms"), st("cvp"), st("cvs"), st("hp"), st("hs"))
```

```python
import functools
import math

import jax
import jax.numpy as jnp
from jax import lax
from jax.experimental import pallas as pl
from jax.experimental.pallas import tpu as pltpu

D_MODEL = 4096
DEPTH = 2
PAGE_SIZE = 128
D_ATTN = D_MODEL // 2
D_MLSTM = D_MODEL // 4
D_RG = D_MODEL - D_ATTN - D_MLSTM
DH_DA = 128
H_DA = D_ATTN // (2 * DH_DA)
H_M = 4
DH_M = D_MLSTM // H_M
RG_BLOCKS = 8
RG_BW = D_RG // RG_BLOCKS
CONV_W = 4
RG_C = 8.0
D_FF = 4 * D_MODEL
ROPE_THETA = 10000.0
MLSTM_CHUNK = 128
NORM_EPS = 1e-6
SUBLN_EPS = 1e-5
NEG = -1e30
D_MAIN = 3 * D_ATTN + 4 * D_MLSTM
ATTN_QSCALE = DH_DA ** -0.5 * math.log2(math.e)

MXU_COLS = 256
LANES = 128
SUBLANES = 8
VMEM_LIMIT = 56 * 1024 * 1024

F32 = jnp.float32
BF16 = jnp.bfloat16


def _params(sem):
    return pltpu.CompilerParams(dimension_semantics=sem, vmem_limit_bytes=VMEM_LIMIT)


def _sigmoid(x):
    return 1.0 / (1.0 + jnp.exp(-x))


def _softplus(x):
    return jnp.maximum(x, 0.0) + jnp.log1p(jnp.exp(-jnp.abs(x)))


def _log_sigmoid(x):
    return -_softplus(-x)


def _gelu_tanh(x):
    c = math.sqrt(2.0 / math.pi)
    return 0.5 * x * (1.0 + jnp.tanh(c * (x + 0.044715 * (x * x * x))))


def _lambda(lq1, lk1, lq2, lk2, lam_init):
    a = jnp.exp(jnp.sum(lq1[...] * lk1[...], axis=-1, keepdims=True))
    b = jnp.exp(jnp.sum(lq2[...] * lk2[...], axis=-1, keepdims=True))
    return a - b + lam_init


def _rotate(x, cos, sin_signed):
    return x * cos + pltpu.roll(x, DH_DA // 2, axis=1) * sin_signed


def _rmsnorm_kernel(x_ref, g_ref, o_ref, *, eps):
    x = x_ref[...].astype(F32)
    ms = jnp.mean(x * x, axis=-1, keepdims=True)
    o_ref[...] = (x * lax.rsqrt(ms + eps) * g_ref[...]).astype(o_ref.dtype)


def _rmsnorm(x, g, eps, out_dtype, tm):
    m, d = x.shape
    return pl.pallas_call(
        functools.partial(_rmsnorm_kernel, eps=eps),
        grid=(m // tm,),
        in_specs=[pl.BlockSpec((tm, d), lambda i: (i, 0)),
                  pl.BlockSpec((1, d), lambda i: (0, 0))],
        out_specs=pl.BlockSpec((tm, d), lambda i: (i, 0)),
        out_shape=jax.ShapeDtypeStruct((m, d), out_dtype),
        compiler_params=_params(("parallel",)),
        name="rmsnorm",
    )(x, g.reshape(1, d).astype(F32))


def _mm_kernel(*refs, nk, seg, w_t, n_out, copy_groups, has_res, has_rope, rope_scale, act, tn,
               has_scale, norm_next):
    it = iter(refs)
    x_refs = [next(it) for _ in seg]
    w_ref = next(it)
    cos_ref = next(it) if has_rope else None
    sin_ref = next(it) if has_rope else None
    res_ref = next(it) if has_res else None
    ss_ref = next(it) if has_scale else None
    g_ref = next(it) if norm_next else None
    copy_in = [[next(it) for _ in range(depth)] for depth in copy_groups]
    o_refs = [next(it) for _ in range(n_out)]
    xg_ref = next(it) if norm_next else None
    ss_out = next(it) if norm_next else None
    copy_out = [next(it) for _ in copy_groups]

    for srcs, dst in zip(copy_in, copy_out):
        for d, src in enumerate(srcs):
            dst[d] = src[...]

    def emit_norm_inputs(x_new):
        xg_ref[...] = (x_new * g_ref[...]).astype(xg_ref.dtype)
        rows = jnp.broadcast_to(jnp.sum(x_new * x_new, axis=1, keepdims=True), ss_out.shape)
        j = pl.program_id(1)

        @pl.when(j == 0)
        def _():
            ss_out[...] = rows

        @pl.when(j > 0)
        def _():
            ss_out[...] += rows

    if nk > 1:
        o = o_refs[0]
        k = pl.program_id(2)
        chunks = [slice(c, c + MXU_COLS) for c in range(0, tn, MXU_COLS)]

        @pl.when(k == 0)
        def _():
            for sl in chunks:
                d = jnp.dot(x_refs[0][...], w_ref[:, sl], preferred_element_type=F32)
                o[:, sl] = res_ref[:, sl] + d if has_res else d

        @pl.when(k > 0)
        def _():
            for sl in chunks:
                o[:, sl] += jnp.dot(x_refs[0][...], w_ref[:, sl], preferred_element_type=F32)

        if norm_next:
            pl.when(k == nk - 1)(lambda: emit_norm_inputs(o[...]))
        return

    if w_t:
        part = lax.dot_general(x_refs[0][...], w_ref[...], (((1,), (1,)), ((), ())),
                               preferred_element_type=F32)
    elif len(seg) == 1:
        part = jnp.dot(x_refs[0][...].astype(BF16), w_ref[...], preferred_element_type=F32)
    else:
        part, off = None, 0
        for x_ref, width in zip(x_refs, seg):
            d = jnp.dot(x_ref[...], w_ref[off:off + width, :], preferred_element_type=F32)
            part = d if part is None else part + d
            off += width

    def store(sl, val):
        for o in o_refs:
            o[:, sl] = val.astype(o.dtype)

    acc = part
    if has_scale:
        acc = acc * lax.rsqrt(ss_ref[:, 0:1] * (1.0 / D_MODEL) + NORM_EPS)
    if has_rope:
        cos = cos_ref[...]
        sin = sin_ref[...]
        for g in range(tn // DH_DA):
            sl = slice(g * DH_DA, (g + 1) * DH_DA)
            store(sl, _rotate(acc[:, sl], cos, sin) * rope_scale)
        return
    if act == "relu2":
        r = jnp.maximum(acc, 0.0)
        acc = r * r
    if has_res:
        acc = res_ref[...] + acc
    store(slice(None), acc)
    if norm_next:
        emit_norm_inputs(acc)


def _matmul(xs, w, *, tm, tn, tk, out_dtypes, n=None, w_col0=0, w_t=False, residual=None, rope=None,
            rope_scale=1.0, act=None, row_ss=None, next_gain=None, stack=(), name="matmul"):
    xs = list(xs) if isinstance(xs, (list, tuple)) else [xs]
    seg = [x.shape[1] for x in xs]
    m = xs[0].shape[0]
    kdim = sum(seg)
    n = w.shape[0 if w_t else 1] - w_col0 if n is None else n
    nk = kdim // tk
    ni, nj = m // tm, n // tn
    assert len(xs) == 1 or (nk == 1 and not w_t)
    if nk > 1:
        assert (not w_t and rope is None and act is None and row_ss is None and not stack
                and tuple(out_dtypes) == (F32,) and xs[0].dtype == BF16)
    jb = w_col0 // tn

    if len(xs) == 1:
        in_specs = [pl.BlockSpec((tm, tk), lambda i, j, k: (i, k))]
    else:
        in_specs = [pl.BlockSpec((tm, s), lambda i, j, k: (i, 0)) for s in seg]
    if w_t:
        in_specs.append(pl.BlockSpec((tn, tk), lambda i, j, k: (j + jb, k)))
    else:
        in_specs.append(pl.BlockSpec((tk, tn), lambda i, j, k: (k, j + jb)))
    args = xs + [w]
    if rope is not None:
        nr = rope[0].shape[0] // tm
        spec = pl.BlockSpec((tm, DH_DA), lambda i, j, k: (i % nr, 0))
        in_specs += [spec, spec]
        args += [rope[0], rope[1]]
    if residual is not None:
        in_specs.append(pl.BlockSpec((tm, tn), lambda i, j, k: (i, j)))
        args.append(residual)
    if row_ss is not None:
        in_specs.append(pl.BlockSpec((tm, LANES), lambda i, j, k: (i, 0)))
        args.append(row_ss)
    if next_gain is not None:
        in_specs.append(pl.BlockSpec((1, tn), lambda i, j, k: (0, j)))
        args.append(next_gain.reshape(1, n).astype(F32))
    out_specs = [pl.BlockSpec((tm, tn), lambda i, j, k: (i, j)) for _ in out_dtypes]
    out_shapes = [jax.ShapeDtypeStruct((m, n), dt) for dt in out_dtypes]
    if next_gain is not None:
        out_specs += [pl.BlockSpec((tm, tn), lambda i, j, k: (i, j)),
                      pl.BlockSpec((tm, LANES), lambda i, j, k: (i, 0))]
        out_shapes += [jax.ShapeDtypeStruct((m, n), BF16), jax.ShapeDtypeStruct((m, LANES), F32)]
    slab = m // (ni * nj)
    for group in stack:
        width = group[0].shape[1]
        for arr in group:
            in_specs.append(pl.BlockSpec((slab, width), lambda i, j, k: (i * nj + j, 0)))
            args.append(arr)
        out_specs.append(pl.BlockSpec((len(group), slab, width), lambda i, j, k: (0, i * nj + j, 0)))
        out_shapes.append(jax.ShapeDtypeStruct((len(group), m, width), group[0].dtype))
    outs = pl.pallas_call(
        functools.partial(_mm_kernel, nk=nk, seg=seg, w_t=w_t, n_out=len(out_dtypes),
                          copy_groups=[len(g) for g in stack],
                          has_res=residual is not None, has_rope=rope is not None, rope_scale=rope_scale,
                          act=act, tn=tn, has_scale=row_ss is not None, norm_next=next_gain is not None),
        grid=(ni, nj, nk),
        in_specs=in_specs,
        out_specs=out_specs,
        out_shape=out_shapes,
        compiler_params=_params(("parallel", "arbitrary", "arbitrary")),
        name=name,
    )(*args)
    return outs[0] if len(outs) == 1 else outs


def _cast_mm_kernel(*refs, nk, has_res, act):
    it = iter(refs)
    x_ref = next(it)
    w_ref = next(it)
    res_ref = next(it) if has_res else None
    wb_ref = next(it)
    y_ref = next(it)
    k = pl.program_id(1)
    wb = w_ref[...].astype(BF16)
    wb_ref[...] = wb
    part = jnp.dot(x_ref[...].astype(BF16), wb, preferred_element_type=F32)

    @pl.when(k == 0)
    def _():
        y_ref[...] = part

    @pl.when(k > 0)
    def _():
        y_ref[...] += part

    if act is not None or has_res:
        @pl.when(k == nk - 1)
        def _():
            acc = y_ref[...]
            if act == "relu2":
                r = jnp.maximum(acc, 0.0)
                acc = r * r
            if has_res:
                acc = res_ref[...] + acc
            y_ref[...] = acc


def _cast_matmul(x, w_all, layer, n, *, tn, tk, residual=None, act=None, name="cast_matmul"):
    m, kdim = x.shape
    nk = kdim // tk
    in_specs = [pl.BlockSpec((m, tk), lambda j, k: (0, k)),
                pl.BlockSpec((None, tk, tn), lambda j, k: (layer, k, j))]
    args = [x, w_all]
    if residual is not None:
        in_specs.append(pl.BlockSpec((m, tn), lambda j, k: (0, j)))
        args.append(residual)
    wb, y = pl.pallas_call(
        functools.partial(_cast_mm_kernel, nk=nk, has_res=residual is not None, act=act),
        grid=(n // tn, nk),
        in_specs=in_specs,
        out_specs=[pl.BlockSpec((tk, tn), lambda j, k: (k, j)),
                   pl.BlockSpec((m, tn), lambda j, k: (0, j))],
        out_shape=[jax.ShapeDtypeStruct((kdim, n), BF16), jax.ShapeDtypeStruct((m, n), F32)],
        compiler_params=_params(("parallel", "arbitrary")),
        name=name,
    )(*args)
    return y, wb


def _nt_dot(x, wt):
    return lax.dot_general(x, wt, (((1,), (1,)), ((), ())), preferred_element_type=F32)


def _cast_mm_t_kernel(x_ref, w_ref, wb_ref, y_ref):
    wb = w_ref[...].astype(BF16)
    wb_ref[...] = wb
    y_ref[...] = _nt_dot(x_ref[...].astype(BF16), wb)


def _cast_matmul_t(x, wt_all, layer, n, *, tn, name):
    m, kdim = x.shape
    wb, y = pl.pallas_call(
        _cast_mm_t_kernel,
        grid=(n // tn,),
        in_specs=[pl.BlockSpec((m, kdim), lambda j: (0, 0)),
                  pl.BlockSpec((None, tn, kdim), lambda j: (layer, j, 0))],
        out_specs=[pl.BlockSpec((tn, kdim), lambda j: (j, 0)),
                   pl.BlockSpec((m, tn), lambda j: (0, j))],
        out_shape=[jax.ShapeDtypeStruct((n, kdim), BF16), jax.ShapeDtypeStruct((m, n), F32)],
        compiler_params=_params(("parallel",)),
        name=name,
    )(x, wt_all)
    return y, wb


def _cast_tail_kernel(x_ref, a_ref, b_ref, wr_ref, wg_ref, yr_ref, yg_ref):
    j = pl.program_id(0)
    xb = x_ref[...].astype(BF16)
    a = a_ref[...]
    tile = jnp.concatenate([a[SUBLANES:], b_ref[...]], axis=0).astype(BF16)
    wr_ref[...] = tile
    yr_ref[...] = _nt_dot(xb, tile)

    @pl.when(j == 0)
    def _():
        pad = jnp.zeros((LANES - SUBLANES, a.shape[1]), F32)
        g = jnp.concatenate([a[:SUBLANES], pad], axis=0).astype(BF16)
        wg_ref[...] = g
        yg_ref[...] = _nt_dot(xb, g)


def _cast_tail(x, wt_all, layer, *, tn, name):
    assert 2 * H_M == SUBLANES
    m, kdim = x.shape
    n = 2 * D_RG
    blk0 = D_MAIN // tn
    return pl.pallas_call(
        _cast_tail_kernel,
        grid=(n // tn,),
        in_specs=[pl.BlockSpec((m, kdim), lambda j: (0, 0)),
                  pl.BlockSpec((None, tn, kdim), lambda j: (layer, blk0 + j, 0)),
                  pl.BlockSpec((None, SUBLANES, kdim), lambda j: (layer, (blk0 + j + 1) * (tn // SUBLANES), 0))],
        out_specs=[pl.BlockSpec((tn, kdim), lambda j: (j, 0)),
                   pl.BlockSpec((LANES, kdim), lambda j: (0, 0)),
                   pl.BlockSpec((m, tn), lambda j: (0, j)),
                   pl.BlockSpec((m, LANES), lambda j: (0, 0))],
        out_shape=[jax.ShapeDtypeStruct((n, kdim), BF16), jax.ShapeDtypeStruct((LANES, kdim), BF16),
                   jax.ShapeDtypeStruct((m, n), F32), jax.ShapeDtypeStruct((m, LANES), F32)],
        compiler_params=_params(("arbitrary",)),
        name=name,
    )(x, wt_all, wt_all)


def _rope_rows_kernel(x_ref, cos_ref, sin_ref, o_ref, *, n_q):
    cos = cos_ref[...]
    sin = sin_ref[...]
    for g in range(x_ref.shape[1] // DH_DA):
        sl = slice(g * DH_DA, (g + 1) * DH_DA)
        y = _rotate(x_ref[:, sl], cos, sin)
        o_ref[:, sl] = y * ATTN_QSCALE if g < n_q else y


def _rope_rows(x, cos, sin):
    m, n = x.shape
    full = lambda c: pl.BlockSpec((m, c), lambda i: (0, 0))
    return pl.pallas_call(
        functools.partial(_rope_rows_kernel, n_q=D_ATTN // DH_DA),
        grid=(1,),
        in_specs=[full(n), full(DH_DA), full(DH_DA)],
        out_specs=full(n),
        out_shape=jax.ShapeDtypeStruct((m, n), F32),
        compiler_params=_params(("arbitrary",)),
        name="rope_sample",
    )(x, cos, sin)


def _attn_prompt_kernel(lq1, lk1, lq2, lk2, g_ref, q_ref, k_ref, v_ref, o_ref, *, t, nq, hp, lam_init):
    qi = pl.program_id(2)
    dh2 = 2 * DH_DA
    lam = _lambda(lq1, lk1, lq2, lk2, lam_init)
    tri = (lax.broadcasted_iota(jnp.int32, (t, t), 1) <= lax.broadcasted_iota(jnp.int32, (t, t), 0))

    def head(n_below, h):
        spans = ([(0, n_below * t, False)] if n_below else []) + [(n_below * t, (n_below + 1) * t, True)]
        probs, inv = [], []
        for c in range(2):
            sl = slice(h * dh2 + c * DH_DA, h * dh2 + (c + 1) * DH_DA)
            ss = []
            for lo, hi, diagonal in spans:
                s = lax.dot_general(q_ref[:, sl], k_ref[lo:hi, sl], (((1,), (1,)), ((), ())),
                                    preferred_element_type=F32)
                ss.append(jnp.where(tri, s, NEG) if diagonal else s)
            m = functools.reduce(jnp.maximum, [jnp.max(s, axis=1, keepdims=True) for s in ss])
            ps = [jnp.exp2(s - m) for s in ss]
            probs.append(ps)
            inv.append(1.0 / functools.reduce(jnp.add, [jnp.sum(p, axis=1, keepdims=True) for p in ps]))
        r1 = inv[0]
        r2 = lam * inv[1]
        hs = slice(h * dh2, (h + 1) * dh2)
        out = None
        for i, (lo, hi, _) in enumerate(spans):
            a = (probs[0][i] * r1 - probs[1][i] * r2).astype(BF16)
            d = jnp.dot(a, v_ref[lo:hi, hs], preferred_element_type=F32)
            out = d if out is None else out + d
        ms = jnp.mean(out * out, axis=-1, keepdims=True)
        y = out * lax.rsqrt(ms + SUBLN_EPS) * g_ref[...]
        o_ref[:, hs] = (y * (1.0 - lam_init)).astype(o_ref.dtype)

    def block(n_below):
        for h in range(hp):
            head(n_below, h)

    for n_below in range(nq):
        pl.when(qi == n_below)(functools.partial(block, n_below))


def _attn_prompt(q, k, v, lam_rows, subln_g, lam_init, batch, seq, t=512, hp=4):
    nq = seq // t
    dh2 = 2 * DH_DA
    row_spec = pl.BlockSpec((1, DH_DA), lambda b, h, i: (0, 0))
    return pl.pallas_call(
        functools.partial(_attn_prompt_kernel, t=t, nq=nq, hp=hp, lam_init=lam_init),
        grid=(batch, H_DA // hp, nq),
        in_specs=[row_spec, row_spec, row_spec, row_spec,
                  pl.BlockSpec((1, dh2), lambda b, h, i: (0, 0)),
                  pl.BlockSpec((t, hp * dh2), lambda b, h, i: (b * nq + i, h)),
                  pl.BlockSpec((seq, hp * dh2), lambda b, h, i: (b, h)),
                  pl.BlockSpec((seq, hp * dh2), lambda b, h, i: (b, h))],
        out_specs=pl.BlockSpec((t, hp * dh2), lambda b, h, i: (b * nq + i, h)),
        out_shape=jax.ShapeDtypeStruct((batch * seq, D_ATTN), BF16),
        compiler_params=_params(("parallel", "parallel", "arbitrary")),
        name="attn_prompt",
    )(*lam_rows, subln_g.reshape(1, dh2), q, k, v)


def _attn_sample_kernel(*refs, lam_init, n_steps, group):
    pt_ref, lq1, lk1, lq2, lk2, g_ref, q_ref, kn_ref, vn_ref = refs[:9]
    k_refs = refs[9:9 + group]
    v_refs = refs[9 + group:9 + 2 * group]
    o_ref, m_sc, l_sc, acc_sc = refs[9 + 2 * group:]
    del pt_ref
    step = pl.program_id(1)
    rows = PAGE_SIZE * H_DA

    @pl.when(step == 0)
    def _():
        m_sc[...] = jnp.full(m_sc.shape, NEG, F32)
        l_sc[...] = jnp.zeros(l_sc.shape, F32)
        acc_sc[...] = jnp.zeros(acc_sc.shape, F32)

    q = q_ref[...]
    zero = jnp.zeros((H_DA, DH_DA), F32)
    qbd = jnp.concatenate([jnp.concatenate([q[:, :DH_DA], zero], axis=1),
                           jnp.concatenate([zero, q[:, DH_DA:]], axis=1)], axis=0).astype(BF16)
    own_head = (lax.broadcasted_iota(jnp.int32, (2 * H_DA, rows), 0) % H_DA
                == lax.broadcasted_iota(jnp.int32, (2 * H_DA, rows), 1) % H_DA)
    scores = []
    for g in range(group):
        k2 = k_refs[g][...].reshape(rows, 2 * DH_DA).astype(BF16)
        s = lax.dot_general(qbd, k2, (((1,), (1,)), ((), ())), preferred_element_type=F32)
        scores.append(jnp.where(own_head, s, NEG))
    m_prev = m_sc[...]
    m_new = m_prev
    for s in scores:
        m_new = jnp.maximum(m_new, jnp.max(s, axis=1, keepdims=True))
    alpha = jnp.exp2(m_prev - m_new)
    l_new = alpha * l_sc[...]
    acc = alpha * acc_sc[...]
    for g in range(group):
        p = jnp.exp2(scores[g] - m_new)
        l_new = l_new + jnp.sum(p, axis=1, keepdims=True)
        v2 = v_refs[g][...].reshape(rows, 2 * DH_DA).astype(BF16)
        acc = acc + jnp.dot(p.astype(BF16), v2, preferred_element_type=F32)
    m_sc[...] = m_new
    l_sc[...] = l_new
    acc_sc[...] = acc

    @pl.when(step == n_steps - 1)
    def _():
        kn = kn_ref[...]
        vn = vn_ref[...]
        pn = kn * q
        sn = jnp.concatenate([jnp.sum(pn[:, :DH_DA], axis=-1, keepdims=True),
                              jnp.sum(pn[:, DH_DA:], axis=-1, keepdims=True)], axis=0)
        m_fin = jnp.maximum(m_new, sn)
        pe = jnp.exp2(sn - m_fin)
        a_fin = jnp.exp2(m_new - m_fin)
        l_fin = a_fin * l_new + pe
        outs = (a_fin * acc + pe * jnp.concatenate([vn, vn], axis=0)) / l_fin
        lam = _lambda(lq1, lk1, lq2, lk2, lam_init)
        out = outs[0:H_DA] - lam * outs[H_DA:]
        ms = jnp.mean(out * out, axis=-1, keepdims=True)
        y = out * lax.rsqrt(ms + SUBLN_EPS) * g_ref[...]
        o_ref[...] = (y * (1.0 - lam_init)).astype(o_ref.dtype)


def _attn_sample(page_table, q, k_new, v_new, cache_k, cache_v, layer, lam_rows, subln_g, lam_init, group=8):
    bd, n_pages = page_table.shape
    n_steps = n_pages // group
    dh2 = 2 * DH_DA
    row_spec = pl.BlockSpec((1, DH_DA), lambda b, p, pt: (0, 0))
    tok_spec = pl.BlockSpec((None, H_DA, dh2), lambda b, p, pt: (b, 0, 0))

    def page_spec(g):
        return pl.BlockSpec((None, None, PAGE_SIZE, H_DA, dh2),
                            lambda b, p, pt: (layer, pt[b, p * group + g], 0, 0, 0))

    pages = [page_spec(g) for g in range(group)]
    grid_spec = pltpu.PrefetchScalarGridSpec(
        num_scalar_prefetch=1,
        grid=(bd, n_steps),
        in_specs=[row_spec, row_spec, row_spec, row_spec,
                  pl.BlockSpec((1, dh2), lambda b, p, pt: (0, 0)),
                  tok_spec, tok_spec, tok_spec] + pages + pages,
        out_specs=tok_spec,
        scratch_shapes=[pltpu.VMEM((2 * H_DA, 1), F32), pltpu.VMEM((2 * H_DA, 1), F32),
                        pltpu.VMEM((2 * H_DA, dh2), F32)],
    )
    return pl.pallas_call(
        functools.partial(_attn_sample_kernel, lam_init=lam_init, n_steps=n_steps, group=group),
        grid_spec=grid_spec,
        out_shape=jax.ShapeDtypeStruct((bd, H_DA, dh2), F32),
        compiler_params=_params(("parallel", "arbitrary")),
        name="attn_sample",
    )(page_table, *lam_rows, subln_g.reshape(1, dh2), q, k_new, v_new,
      *([cache_k] * group), *([cache_v] * group))


def _mlstm_out(h, g, om):
    mu = jnp.mean(h, axis=-1, keepdims=True)
    hc = h - mu
    var = jnp.mean(hc * hc, axis=-1, keepdims=True)
    return hc * lax.rsqrt(var + NORM_EPS) * g * _sigmoid(om)


def _mlstm_prompt_kernel(big_ref, bfg_ref, q_ref, k_ref, v_ref, om_ref, gc_ref, gr_ref, ng_ref,
                         hm_ref, c_out, n_out, m_out, c_sc, n_sc, m_sc, *, n_chunks):
    ci = pl.program_id(1)
    L = MLSTM_CHUNK

    @pl.when(ci == 0)
    def _():
        c_sc[...] = jnp.zeros(c_sc.shape, F32)
        n_sc[...] = jnp.zeros(n_sc.shape, F32)
        m_sc[...] = jnp.zeros(m_sc.shape, F32)

    ri = lax.broadcasted_iota(jnp.int32, (L, L), 0)
    cj = lax.broadcasted_iota(jnp.int32, (L, L), 1)
    lower = ri >= cj
    gc = gc_ref[...]
    gr = gr_ref[...]
    for h in range(H_M):
        hs = slice(h * DH_M, (h + 1) * DH_M)
        ig_c = gc[:, h:h + 1] + big_ref[h]
        lf_c = _log_sigmoid(gc[:, H_M + h:H_M + h + 1] + bfg_ref[h])
        ig_r = gr[h:h + 1, :] + big_ref[h]
        lf_r = _log_sigmoid(gr[H_M + h:H_M + h + 1, :] + bfg_ref[h])
        b_col = jnp.sum(jnp.where(lower, lf_r, 0.0), axis=1, keepdims=True)
        b_row = jnp.sum(jnp.where(cj >= ri, lf_c, 0.0), axis=0, keepdims=True)
        dm = jnp.where(lower, b_col - b_row + ig_r, -jnp.inf)
        m_prev = m_sc[h:h + 1, 0:1]
        inter = b_col + m_prev
        m_t = jnp.maximum(inter, jnp.max(dm, axis=1, keepdims=True))
        w_intra = jnp.exp(dm - m_t)
        w_inter = jnp.exp(inter - m_t)
        q = q_ref[:, hs]
        ks = k_ref[:, hs] * (DH_M ** -0.5)
        v = v_ref[:, hs]
        c_prev = c_sc[h]
        n_prev = n_sc[h:h + 1, :]
        sc = w_intra * lax.dot_general(q, ks, (((1,), (1,)), ((), ())), preferred_element_type=F32)
        num = (jnp.dot(sc.astype(BF16), v, preferred_element_type=F32)
               + w_inter * lax.dot_general(q, c_prev.astype(BF16), (((1,), (1,)), ((), ())),
                                           preferred_element_type=F32))
        den = (jnp.sum(sc, axis=1, keepdims=True)
               + w_inter * jnp.sum(q.astype(F32) * n_prev, axis=1, keepdims=True))
        hh = num / jnp.maximum(jnp.abs(den), jnp.exp(-m_t))
        m_new = m_t[L - 1:L, :]
        decay = w_inter[L - 1:L, :]
        wk = jnp.exp(b_col[L - 1:L, :] - b_col + ig_c - m_new)
        wv = (wk * v.astype(F32)).astype(BF16)
        c_sc[h] = decay * c_prev + lax.dot_general(wv, ks, (((0,), (0,)), ((), ())),
                                                   preferred_element_type=F32)
        n_sc[h:h + 1, :] = decay * n_prev + jnp.sum(wk * ks.astype(F32), axis=0, keepdims=True)
        m_sc[h:h + 1, :] = jnp.broadcast_to(m_new, (1, LANES))
        hm_ref[:, hs] = _mlstm_out(hh, ng_ref[:, hs], om_ref[:, hs].astype(F32)).astype(hm_ref.dtype)

    @pl.when(ci == n_chunks - 1)
    def _():
        c_out[...] = c_sc[...]
        n_out[...] = n_sc[0:H_M, :]
        m_out[...] = m_sc[0:H_M, :]


def _mlstm_prompt(um, gates, gates_t, b_ig, b_fg, norm_g, batch, seq):
    L = MLSTM_CHUNK
    nc = seq // L
    smem = pl.BlockSpec(memory_space=pltpu.SMEM)

    def col(cb):
        return pl.BlockSpec((L, D_MLSTM), lambda b, c: (b * nc + c, cb))

    return pl.pallas_call(
        functools.partial(_mlstm_prompt_kernel, n_chunks=nc),
        grid=(batch, nc),
        in_specs=[smem, smem, col(0), col(1), col(2), col(3),
                  pl.BlockSpec((L, LANES), lambda b, c: (b * nc + c, 0)),
                  pl.BlockSpec((None, SUBLANES, L), lambda b, c: (b, 0, c)),
                  pl.BlockSpec((1, D_MLSTM), lambda b, c: (0, 0))],
        out_specs=[pl.BlockSpec((L, D_MLSTM), lambda b, c: (b * nc + c, 0)),
                   pl.BlockSpec((None, H_M, DH_M, DH_M), lambda b, c: (b, 0, 0, 0)),
                   pl.BlockSpec((None, H_M, DH_M), lambda b, c: (b, 0, 0)),
                   pl.BlockSpec((None, H_M, LANES), lambda b, c: (b, 0, 0))],
        out_shape=[jax.ShapeDtypeStruct((batch * seq, D_MLSTM), BF16),
                   jax.ShapeDtypeStruct((batch, H_M, DH_M, DH_M), F32),
                   jax.ShapeDtypeStruct((batch, H_M, DH_M), F32),
                   jax.ShapeDtypeStruct((batch, H_M, LANES), F32)],
        scratch_shapes=[pltpu.VMEM((H_M, DH_M, DH_M), F32), pltpu.VMEM((SUBLANES, DH_M), F32),
                        pltpu.VMEM((SUBLANES, LANES), F32)],
        compiler_params=_params(("parallel", "arbitrary")),
        name="mlstm_prompt",
    )(b_ig, b_fg, um, um, um, um, gates, gates_t, norm_g.reshape(1, D_MLSTM))


def _mlstm_sample_kernel(big_ref, bfg_ref, u_ref, vcol_ref, g_ref, ng_ref, c_ref, n_ref, m_ref,
                         hm_ref, c_out, n_out, m_out):
    u = u_ref[...]
    g = g_ref[...]
    for h in range(H_M):
        def part(i):
            return u[:, i * D_MLSTM + h * DH_M:i * D_MLSTM + (h + 1) * DH_M]
        q, k, v, om = part(0), part(1), part(2), part(3)
        ks = k * (DH_M ** -0.5)
        ig = g[:, h:h + 1] + big_ref[h]
        lf = _log_sigmoid(g[:, H_M + h:H_M + h + 1] + bfg_ref[h])
        m_prev = m_ref[:, h:h + 1]
        inter = lf + m_prev
        m_t = jnp.maximum(inter, ig)
        w_intra = jnp.exp(ig - m_t)
        w_inter = jnp.exp(inter - m_t)
        c_prev = c_ref[h]
        n_prev = n_ref[h:h + 1, :]
        q8 = jnp.broadcast_to(q, (2 * SUBLANES, DH_M)).astype(BF16)
        cq = lax.dot_general(q8, c_prev.astype(BF16), (((1,), (1,)), ((), ())),
                             preferred_element_type=F32)[0:1, :]
        sc = w_intra * jnp.sum(q * ks, axis=-1, keepdims=True)
        num = sc * v + w_inter * cq
        den = sc + w_inter * jnp.sum(n_prev * q, axis=-1, keepdims=True)
        hh = num / jnp.maximum(jnp.abs(den), jnp.exp(-m_t))
        c_out[h] = w_inter * c_prev + (w_intra * vcol_ref[h]) * ks
        n_out[h:h + 1, :] = w_inter * n_prev + w_intra * ks
        m_out[h:h + 1, :] = jnp.broadcast_to(m_t, (1, LANES))
        hs = slice(h * DH_M, (h + 1) * DH_M)
        hm_ref[:, hs] = _mlstm_out(hh, ng_ref[:, hs], om)


def _mlstm_sample(um, gates, b_ig, b_fg, norm_g, c_state, n_state, m_state):
    bd = um.shape[0]
    smem = pl.BlockSpec(memory_space=pltpu.SMEM)
    u4 = um.reshape(bd, 1, 4 * D_MLSTM)
    vcol = um[:, 2 * D_MLSTM:3 * D_MLSTM].reshape(bd, H_M, DH_M, 1)
    return pl.pallas_call(
        _mlstm_sample_kernel,
        grid=(bd,),
        in_specs=[smem, smem,
                  pl.BlockSpec((None, 1, 4 * D_MLSTM), lambda b: (b, 0, 0)),
                  pl.BlockSpec((None, H_M, DH_M, 1), lambda b: (b, 0, 0, 0)),
                  pl.BlockSpec((None, 1, LANES), lambda b: (b, 0, 0)),
                  pl.BlockSpec((1, D_MLSTM), lambda b: (0, 0)),
                  pl.BlockSpec((None, H_M, DH_M, DH_M), lambda b: (b, 0, 0, 0)),
                  pl.BlockSpec((None, H_M, DH_M), lambda b: (b, 0, 0)),
                  pl.BlockSpec((None, 1, H_M), lambda b: (b, 0, 0))],
        out_specs=[pl.BlockSpec((None, 1, D_MLSTM), lambda b: (b, 0, 0)),
                   pl.BlockSpec((None, H_M, DH_M, DH_M), lambda b: (b, 0, 0, 0)),
                   pl.BlockSpec((None, H_M, DH_M), lambda b: (b, 0, 0)),
                   pl.BlockSpec((None, H_M, LANES), lambda b: (b, 0, 0))],
        out_shape=[jax.ShapeDtypeStruct((bd, 1, D_MLSTM), F32),
                   jax.ShapeDtypeStruct((bd, H_M, DH_M, DH_M), F32),
                   jax.ShapeDtypeStruct((bd, H_M, DH_M), F32),
                   jax.ShapeDtypeStruct((bd, H_M, LANES), F32)],
        compiler_params=_params(("parallel",)),
        name="mlstm_sample",
    )(b_ig, b_fg, u4, vcol, gates.reshape(bd, 1, LANES), norm_g.reshape(1, D_MLSTM),
      c_state, n_state, m_state.reshape(bd, 1, H_M))


def _rglru_coeffs(xc, wra_ref, bra_ref, wrx_ref, brx_ref, lam_ref):
    rs, is_ = [], []
    for n in range(RG_BLOCKS):
        xb = xc[:, n * RG_BW:(n + 1) * RG_BW].astype(BF16)
        rs.append(jnp.dot(xb, wra_ref[n], preferred_element_type=F32))
        is_.append(jnp.dot(xb, wrx_ref[n], preferred_element_type=F32))
    r = _sigmoid(jnp.concatenate(rs, axis=-1) + bra_ref[...])
    i = _sigmoid(jnp.concatenate(is_, axis=-1) + brx_ref[...])
    log_a = -RG_C * r * _softplus(-lam_ref[...])
    a = jnp.exp(log_a)
    mult = jnp.sqrt(1.0 - a * a)
    return a, mult * i * xc


def _rglru_prompt_kernel(x_ref, gate_ref, cw_ref, cb_ref, wra_ref, bra_ref, wrx_ref, brx_ref, lam_ref,
                         y_ref, conv_out, h_out, xbuf, a_sc, b_sc, h_sc, *, tt, n_tiles):
    ti = pl.program_id(1)
    pad = SUBLANES

    @pl.when(ti == 0)
    def _():
        xbuf[0:pad, :] = jnp.zeros((pad, D_RG), F32)
        h_sc[...] = jnp.zeros(h_sc.shape, F32)

    xbuf[pad:pad + tt, :] = x_ref[...].astype(F32)
    xc = cb_ref[...] + sum(xbuf[pad - (CONV_W - 1) + j:pad - (CONV_W - 1) + j + tt, :] * cw_ref[j:j + 1, :]
                           for j in range(CONV_W))
    a, b = _rglru_coeffs(xc, wra_ref, bra_ref, wrx_ref, brx_ref, lam_ref)
    a_sc[...] = a
    b_sc[...] = b

    def step(t, h):
        h = a_sc[pl.ds(t, 1), :] * h + b_sc[pl.ds(t, 1), :]
        b_sc[pl.ds(t, 1), :] = h
        return h

    h_last = lax.fori_loop(0, tt, step, h_sc[...], unroll=8)
    h_sc[...] = h_last
    y_ref[...] = (b_sc[...] * _gelu_tanh(gate_ref[...].astype(F32))).astype(y_ref.dtype)
    tail = xbuf[tt:tt + pad, :]
    xbuf[0:pad, :] = tail

    @pl.when(ti == n_tiles - 1)
    def _():
        conv_out[...] = tail[pad - (CONV_W - 1):, :]
        h_out[...] = h_last


def _rglru_weights(conv_w, conv_b, w_ra, b_ra, w_rx, b_rx, lam):
    row = lambda z: z.reshape(1, D_RG).astype(F32)
    return (conv_w.astype(F32), row(conv_b), w_ra.astype(BF16), row(b_ra), w_rx.astype(BF16), row(b_rx), row(lam))


def _rglru_weight_specs():
    zeros2 = (lambda *a: (0, 0))
    zeros3 = (lambda *a: (0, 0, 0))
    return [pl.BlockSpec((CONV_W, D_RG), zeros2), pl.BlockSpec((1, D_RG), zeros2),
            pl.BlockSpec((RG_BLOCKS, RG_BW, RG_BW), zeros3), pl.BlockSpec((1, D_RG), zeros2),
            pl.BlockSpec((RG_BLOCKS, RG_BW, RG_BW), zeros3), pl.BlockSpec((1, D_RG), zeros2),
            pl.BlockSpec((1, D_RG), zeros2)]


def _rglru_prompt(urg, weights, batch, seq, tt=256):
    nt = seq // tt
    return pl.pallas_call(
        functools.partial(_rglru_prompt_kernel, tt=tt, n_tiles=nt),
        grid=(batch, nt),
        in_specs=[pl.BlockSpec((tt, D_RG), lambda b, t: (b * nt + t, 0)),
                  pl.BlockSpec((tt, D_RG), lambda b, t: (b * nt + t, 1))] + _rglru_weight_specs(),
        out_specs=[pl.BlockSpec((tt, D_RG), lambda b, t: (b * nt + t, 0)),
                   pl.BlockSpec((None, CONV_W - 1, D_RG), lambda b, t: (b, 0, 0)),
                   pl.BlockSpec((None, 1, D_RG), lambda b, t: (b, 0, 0))],
        out_shape=[jax.ShapeDtypeStruct((batch * seq, D_RG), BF16),
                   jax.ShapeDtypeStruct((batch, CONV_W - 1, D_RG), F32),
                   jax.ShapeDtypeStruct((batch, 1, D_RG), F32)],
        scratch_shapes=[pltpu.VMEM((tt + SUBLANES, D_RG), F32), pltpu.VMEM((tt, D_RG), F32),
                        pltpu.VMEM((tt, D_RG), F32), pltpu.VMEM((1, D_RG), F32)],
        compiler_params=_params(("parallel", "arbitrary")),
        name="rglru_prompt",
    )(urg, urg, *weights)


def _rglru_sample_kernel(x_ref, gate_ref, p0_ref, p1_ref, p2_ref, h0_ref, cw_ref, cb_ref, wra_ref, bra_ref,
                         wrx_ref, brx_ref, lam_ref, y_ref, h_out):
    x = x_ref[...]
    xc = cb_ref[...] + (p0_ref[...] * cw_ref[0:1, :] + p1_ref[...] * cw_ref[1:2, :]
                        + p2_ref[...] * cw_ref[2:3, :] + x * cw_ref[3:4, :])
    a, b = _rglru_coeffs(xc, wra_ref, bra_ref, wrx_ref, brx_ref, lam_ref)
    h = a * h0_ref[...] + b
    h_out[...] = h
    y_ref[...] = h * _gelu_tanh(gate_ref[...])


def _rglru_sample(x, gate, conv_prev, h0, weights):
    bd = x.shape[0]
    full = pl.BlockSpec((bd, D_RG), lambda i: (0, 0))
    return pl.pallas_call(
        _rglru_sample_kernel,
        grid=(1,),
        in_specs=[full] * 6 + _rglru_weight_specs(),
        out_specs=[full, full],
        out_shape=[jax.ShapeDtypeStruct((bd, D_RG), F32)] * 2,
        compiler_params=_params(("arbitrary",)),
        name="rglru_sample",
    )(x, gate, conv_prev[:, 0], conv_prev[:, 1], conv_prev[:, 2], h0, *weights)


def _rope_tables(pos):
    inv = 1.0 / (ROPE_THETA ** (jnp.arange(0, DH_DA, 2, dtype=F32) / DH_DA))
    ang = pos.astype(F32)[:, None] * inv[None, :]
    ang = jnp.concatenate([ang, ang], axis=-1)
    sign = jnp.concatenate([-jnp.ones((DH_DA // 2,), F32), jnp.ones((DH_DA // 2,), F32)])
    return jnp.cos(ang), jnp.sin(ang) * sign[None, :]


def kernel(x_prompt, x_sample, cache_k, cache_v, page_table, state_mlstm_c, state_mlstm_n, state_mlstm_m, state_conv, state_rglru_h, norm_mix_g, w_in, w_out, lam_q1, lam_k1, lam_q2, lam_k2, attn_subln_g, b_ig, b_fg, mlstm_norm_g, conv_w, conv_b, w_ra, b_ra, w_rx, b_rx, rg_lambda, norm_mlp_g, w_up, w_down, final_norm_g):
    bp, sp, _ = x_prompt.shape
    bd, td, _ = x_sample.shape
    assert td == 1
    past = page_table.shape[1] * PAGE_SIZE
    mp = bp * sp
    dh2 = 2 * DH_DA
    cos_p, sin_p = _rope_tables(jnp.arange(sp))
    cos_s, sin_s = _rope_tables(jnp.full((bd,), past))

    TM = 1024
    TNC = 512

    xp = x_prompt.reshape(mp, D_MODEL)
    xs = x_sample.reshape(bd, D_MODEL)
    w_in_t = jnp.swapaxes(w_in, 1, 2)
    k_layers, v_layers = [], []
    outs = {n: [] for n in ("ks", "vs", "cp", "np", "mp", "cs", "ns", "ms", "cvp", "cvs", "hp", "hs")}
    for l in range(DEPTH):
        lam_init = 0.8 - 0.6 * math.exp(-0.3 * l)
        lam_rows = [z[l].reshape(1, DH_DA).astype(F32) for z in (lam_q1, lam_k1, lam_q2, lam_k2)]
        rg_w = _rglru_weights(conv_w[l], conv_b[l], w_ra[l], b_ra[l], w_rx[l], b_rx[l], rg_lambda[l])

        hs_n = _rmsnorm(xs, norm_mix_g[l], NORM_EPS, F32, tm=bd)
        u_s, w_in_b = _cast_matmul_t(hs_n, w_in_t, l, D_MAIN, tn=TNC, name="s_proj_in")
        wr, wg, urg_s, gates_s = _cast_tail(hs_n, w_in_t, l, tn=TNC, name="s_proj_tail")
        qk_s = _rope_rows(u_s[:, :2 * D_ATTN], cos_s, sin_s)
        q_s, k_s = qk_s[:, :D_ATTN], qk_s[:, D_ATTN:]
        v_s = u_s[:, 2 * D_ATTN:3 * D_ATTN]
        um_s = u_s[:, 3 * D_ATTN:]

        if l == 0:
            hn, ss = _rmsnorm(xp, norm_mix_g[l], NORM_EPS, BF16, tm=512), None
        mm = functools.partial(_matmul, hn, tm=TM, tk=D_MODEL, w_t=True, row_ss=ss)
        q = mm(w_in_b, n=D_ATTN, w_col0=0, tn=1024, out_dtypes=(BF16,), rope=(cos_p, sin_p),
               rope_scale=ATTN_QSCALE, name="proj_q")
        k32, k16 = mm(w_in_b, n=D_ATTN, w_col0=D_ATTN, tn=512, out_dtypes=(F32, BF16), rope=(cos_p, sin_p),
                      name="proj_k")
        v32, v16 = mm(w_in_b, n=D_ATTN, w_col0=2 * D_ATTN, tn=512, out_dtypes=(F32, BF16), name="proj_v")
        k_layers.append(k32)
        v_layers.append(v32)
        um = mm(w_in_b, n=4 * D_MLSTM, w_col0=3 * D_ATTN, tn=1024, out_dtypes=(BF16,), name="proj_m")
        urg = mm(wr, tn=1024, out_dtypes=(BF16,), name="proj_r")
        gates = mm(wg, tn=LANES, out_dtypes=(F32,), name="proj_g")
        gates_t = gates[:, :SUBLANES].reshape(bp, sp, SUBLANES).transpose(0, 2, 1)
        att = _attn_prompt(q, k16, v16, lam_rows, attn_subln_g[l], lam_init, bp, sp)
        hm, c_p, n_p, m_p = _mlstm_prompt(um, gates, gates_t, b_ig[l], b_fg[l], mlstm_norm_g[l], bp, sp)
        hr, cv_p, h_p = _rglru_prompt(urg, rg_w, bp, sp)

        att_s = _attn_sample(page_table, q_s.reshape(bd, H_DA, dh2), k_s.reshape(bd, H_DA, dh2),
                             v_s.reshape(bd, H_DA, dh2), cache_k, cache_v, l, lam_rows,
                             attn_subln_g[l], lam_init)
        hm_s, c_s, n_s, m_s = _mlstm_sample(um_s, gates_s, b_ig[l], b_fg[l], mlstm_norm_g[l],
                                            state_mlstm_c[l], state_mlstm_n[l], state_mlstm_m[l])
        x_rg = urg_s[:, :D_RG]
        hr_s, h_s = _rglru_sample(x_rg, urg_s[:, D_RG:], state_conv[l], state_rglru_h[l], rg_w)
        cat_s = jnp.concatenate([att_s.reshape(bd, D_ATTN), hm_s.reshape(bd, D_MLSTM), hr_s], axis=1)
        x1s, wo = _cast_matmul(cat_s, w_out, l, D_MODEL, tn=TNC, tk=D_MODEL, residual=xs, name="s_proj_out")
        hs_n2 = _rmsnorm(x1s, norm_mlp_g[l], NORM_EPS, F32, tm=bd)
        act_s, wu = _cast_matmul(hs_n2, w_up, l, D_FF, tn=TNC, tk=D_MODEL, act="relu2", name="s_mlp_up")
        xs, wd = _cast_matmul(act_s, w_down, l, D_MODEL, tn=TNC, tk=D_MODEL, residual=x1s, name="s_mlp_down")

        x1, xg1, ss1 = _matmul([att, hm, hr], wo, tm=TM, tn=512, tk=D_MODEL, out_dtypes=(F32,), residual=xp,
                               next_gain=norm_mlp_g[l], name="proj_out")
        up = functools.partial(_matmul, xg1, wu, tm=TM, tn=1024, tk=D_MODEL, out_dtypes=(BF16,), act="relu2",
                               row_ss=ss1, name="mlp_up")
        down = functools.partial(_matmul, w=wd, tm=TM, tn=1024, tk=D_MODEL, out_dtypes=(F32,), residual=x1,
                                 name="mlp_down")
        if l + 1 < DEPTH:
            xp, hn, ss = down(up(), next_gain=norm_mix_g[l + 1])
        else:
            act, k_all, v_all = up(stack=(k_layers, v_layers))
            xp = down(act)

        outs["ks"].append(k_s.reshape(bd, td, H_DA, dh2))
        outs["vs"].append(v_s.reshape(bd, td, H_DA, dh2))
        outs["cp"].append(c_p)
        outs["np"].append(n_p)
        outs["mp"].append(m_p[:, :, 0])
        outs["cs"].append(c_s)
        outs["ns"].append(n_s)
        outs["ms"].append(m_s[:, :, 0])
        outs["cvp"].append(cv_p)
        outs["cvs"].append(jnp.concatenate([state_conv[l][:, 1:], x_rg[:, None, :]], axis=1))
        outs["hp"].append(h_p.reshape(bp, D_RG))
        outs["hs"].append(h_s)

    y_prompt = _rmsnorm(xp, final_norm_g, NORM_EPS, F32, tm=512).reshape(bp, sp, D_MODEL)
    y_sample = _rmsnorm(xs, final_norm_g, NORM_EPS, F32, tm=bd).reshape(bd, td, D_MODEL)
    st = lambda n: jnp.stack(outs[n])
    return (y_prompt, y_sample, k_all.reshape(DEPTH, bp, sp, H_DA, dh2), v_all.reshape(DEPTH, bp, sp, H_DA, dh2),
            st("ks"), st("vs"), st("cp"), st("np"), st("mp"),
            st("cs"), st("ns"), st("ms"), st("cvp"), st("cvs"), st("hp"), st("hs"))
```
